```python
import math
import functools
import jax
import jax.numpy as jnp
from jax import lax
import numpy as np

D_MODEL = 4096
BATCH = 4
SEQ = 2048
DEPTH = 2
DEC_BATCH = 32
DEC_SEQ = 4
PAST_LEN = 16384
PAGE_SIZE = 128

N_HEADS = 32
N_KV_HEADS = 4
HEAD_DIM = 64
GQA_GROUP = N_HEADS // N_KV_HEADS
D_ATTN = N_HEADS * HEAD_DIM
D_KV = N_KV_HEADS * HEAD_DIM
WINDOW = 128
ATTN_BLOCK = WINDOW
N_BUCKETS = 32
MAX_EXACT = N_BUCKETS // 2
MAX_DISTANCE = WINDOW
D_CONV = D_MODEL // 4
CONV_W = 3
D_POOL = D_MODEL // 4
POOL_WINDOWS = (2, 4, 8, 16)
POOL_GROUP = D_POOL // len(POOL_WINDOWS)
POOL_PAD = max(POOL_WINDOWS) - 1
D_MIX = D_ATTN + D_CONV + D_POOL
D_IN = D_ATTN + 2 * D_KV + 3 * D_CONV + D_POOL
SPLITS = (D_ATTN, D_ATTN + D_KV, D_ATTN + 2 * D_KV, D_ATTN + 2 * D_KV + D_CONV, D_ATTN + 2 * D_KV + 2 * D_CONV, D_ATTN + 2 * D_KV + 3 * D_CONV)
D_FF = D_MODEL * 7 // 2
N_EXPERTS = 8
TOP_K = 2
N_DENSE = (DEPTH + 1) // 2
N_MOE = DEPTH // 2
N_MOD = 6
RMS_EPS = 1e-6

kernel_name = 'hymba_swa_conv_pool_adaln_moe_step'


def rmsnorm(x, g):
    xf = x.astype(jnp.float32)
    y = xf * lax.rsqrt(jnp.mean(xf * xf, axis=-1, keepdims=True) + RMS_EPS) * g.astype(jnp.float32)
    return y.astype(x.dtype)


def t5_bucket(dist):
    n = jnp.maximum(dist, 0)
    nf = jnp.maximum(n, 1).astype(jnp.float32)
    large = MAX_EXACT + (jnp.log(nf / MAX_EXACT) / math.log(MAX_DISTANCE / MAX_EXACT) * (N_BUCKETS - MAX_EXACT)).astype(jnp.int32)
    large = jnp.minimum(large, N_BUCKETS - 1)
    return jnp.where(n < MAX_EXACT, n, large)


def sink_attention(q, k, v, dist, valid, sinks, rel_bias):
    n_q, n_k = dist.shape
    s = jnp.einsum('...qkgd,...skd->...kgqs', q.astype(jnp.float32), k.astype(jnp.float32)) / math.sqrt(HEAD_DIM)
    bias = rel_bias.astype(jnp.float32)[t5_bucket(dist)]
    bias = jnp.transpose(bias, (2, 0, 1)).reshape(N_KV_HEADS, GQA_GROUP, n_q, n_k)
    mask = valid & (dist >= 0) & (dist < WINDOW)
    s = jnp.where(mask, s + bias, -jnp.inf)
    sink = jnp.broadcast_to(sinks.astype(jnp.float32).reshape(N_KV_HEADS, GQA_GROUP, 1, 1), s.shape[:-1] + (1,))
    p = jax.nn.softmax(jnp.concatenate([s, sink], axis=-1), axis=-1)[..., :-1]
    return jnp.einsum('...kgqs,...skd->...qkgd', p, v.astype(jnp.float32))


def swa_prompt(q, k, v, sinks, rel_bias):
    b, t, _ = q.shape
    nb = t // ATTN_BLOCK
    qb = q.reshape(b, nb, ATTN_BLOCK, N_KV_HEADS, GQA_GROUP, HEAD_DIM)
    kb = k.reshape(b, nb, ATTN_BLOCK, N_KV_HEADS, HEAD_DIM)
    vb = v.reshape(b, nb, ATTN_BLOCK, N_KV_HEADS, HEAD_DIM)
    shift = lambda a: jnp.concatenate([jnp.zeros_like(a[:, :1]), a[:, :-1]], axis=1)
    kk = jnp.concatenate([shift(kb), kb], axis=2)
    vv = jnp.concatenate([shift(vb), vb], axis=2)
    dist = ATTN_BLOCK + jnp.arange(ATTN_BLOCK)[:, None] - jnp.arange(2 * ATTN_BLOCK)[None, :]
    valid = ((jnp.arange(nb)[:, None] > 0) | (jnp.arange(2 * ATTN_BLOCK)[None, :] >= ATTN_BLOCK))[:, None, None, None, :]
    o = sink_attention(qb, kk, vv, dist, valid, sinks, rel_bias).reshape(b, t, D_ATTN)
    new_k = k.reshape(b, t, N_KV_HEADS, HEAD_DIM)[:, t - WINDOW:]
    new_v = v.reshape(b, t, N_KV_HEADS, HEAD_DIM)[:, t - WINDOW:]
    return o, new_k, new_v


def swa_sample(q, k, v, cache_k, cache_v, sinks, rel_bias):
    b, s, _ = q.shape
    wb = cache_k.shape[1]
    qs = q.reshape(b, s, N_KV_HEADS, GQA_GROUP, HEAD_DIM)
    kk = jnp.concatenate([cache_k.astype(k.dtype), k.reshape(b, s, N_KV_HEADS, HEAD_DIM)], axis=1)
    vv = jnp.concatenate([cache_v.astype(v.dtype), v.reshape(b, s, N_KV_HEADS, HEAD_DIM)], axis=1)
    dist = jnp.arange(s)[:, None] + wb - jnp.arange(wb + s)[None, :]
    o = sink_attention(qs, kk, vv, dist, jnp.ones((), dtype=bool), sinks, rel_bias).reshape(b, s, D_ATTN)
    return o, kk[:, s:], vv[:, s:]


def short_conv(u, prev, w):
    s = u.shape[1]
    ue = jnp.concatenate([prev.astype(u.dtype), u], axis=1)
    y = sum(w[j] * ue[:, j:j + s] for j in range(CONV_W))
    return y, ue[:, s:]


def multi_pool(v, prev, pos0, w_lin, scale):
    b, s, _ = v.shape
    ve = jnp.concatenate([prev.astype(v.dtype), v], axis=1)
    vf = ve.astype(jnp.float32)
    cs = jnp.concatenate([jnp.zeros((b, 1, D_POOL), jnp.float32), jnp.cumsum(vf, axis=1)], axis=1)
    end = cs[:, POOL_PAD + 1:]
    pos = pos0 + jnp.arange(s)
    outs = []
    for gi, w in enumerate(POOL_WINDOWS):
        sl = slice(gi * POOL_GROUP, (gi + 1) * POOL_GROUP)
        start = cs[:, POOL_PAD + 1 - w:POOL_PAD + 1 - w + s, sl]
        cnt = jnp.minimum(pos + 1, w).astype(jnp.float32)[None, :, None]
        outs.append((end[..., sl] - start) / cnt - vf[:, POOL_PAD:, sl])
    g = jnp.stack(outs, axis=2)
    y = jnp.einsum('bsni,nio->bsno', g, w_lin.astype(jnp.float32)).reshape(b, s, D_POOL) * scale.astype(jnp.float32)
    return y.astype(v.dtype), ve[:, s:]


def swiglu(h, wg, wu, wd):
    return (jax.nn.silu(h @ wg) * (h @ wu)) @ wd


def moe_ffn(h, w_router, wg, wu, wd):
    logits = jnp.einsum('bsd,de->bse', h.astype(jnp.float32), w_router.astype(jnp.float32))
    top_v, top_i = lax.top_k(logits, TOP_K)
    top_w = jax.nn.softmax(top_v, axis=-1)
    combine = jnp.sum(jax.nn.one_hot(top_i, N_EXPERTS, dtype=jnp.float32) * top_w[..., None], axis=-2)
    out = jnp.zeros(h.shape, jnp.float32)
    for e in range(N_EXPERTS):
        out = out + combine[..., e:e + 1] * swiglu(h, wg[e], wu[e], wd[e]).astype(jnp.float32)
    return out.astype(h.dtype)


def block(x, c, pos0, cache_k_l, cache_v_l, conv_prev, pool_prev, w_ada_l, b_ada_l, norm_g_l, w_in_l, sinks_l, rel_bias, conv_w_l, pool_w_l, pool_scale_l, w_out_l, ffn):
    mod = (jax.nn.silu(c) @ w_ada_l + b_ada_l)[:, None, :]
    sh1, sc1, g1, sh2, sc2, g2 = jnp.split(mod, N_MOD, axis=-1)
    h = rmsnorm(x, norm_g_l[0]) * (1 + sc1) + sh1
    q, k, v, gate_b, gate_c, xt, pv = jnp.split(h @ w_in_l, SPLITS, axis=-1)
    if cache_k_l is None:
        a, new_k, new_v = swa_prompt(q, k, v, sinks_l, rel_bias)
    else:
        a, new_k, new_v = swa_sample(q, k, v, cache_k_l, cache_v_l, sinks_l, rel_bias)
    cy, new_conv = short_conv(gate_c * xt, conv_prev, conv_w_l)
    po, new_pool = multi_pool(pv, pool_prev, pos0, pool_w_l, pool_scale_l)
    mix = jnp.concatenate([a.astype(x.dtype), gate_b * cy, po], axis=-1) @ w_out_l
    x = x + g1 * rmsnorm(mix, norm_g_l[1])
    h = rmsnorm(x, norm_g_l[2]) * (1 + sc2) + sh2
    x = x + g2 * rmsnorm(ffn(h), norm_g_l[3])
    return x, new_k, new_v, new_conv, new_pool


def setup_inputs(seed: int = 0) -> dict:
    key = jax.random.key(seed)
    ks = jax.random.split(key, 32)
    nrm = lambda k, shape, scale: jax.random.normal(k, shape, jnp.float32) * scale
    d = D_MODEL
    w_buf = min(WINDOW, PAST_LEN)
    return {
        'x_prompt': nrm(ks[0], (BATCH, SEQ, d), 1.0),
        'x_sample': nrm(ks[1], (DEC_BATCH, DEC_SEQ, d), 1.0),
        'cache_k': nrm(ks[2], (DEPTH, DEC_BATCH, w_buf, N_KV_HEADS, HEAD_DIM), 1.0),
        'cache_v': nrm(ks[3], (DEPTH, DEC_BATCH, w_buf, N_KV_HEADS, HEAD_DIM), 1.0),
        'state_conv': nrm(ks[4], (DEPTH, DEC_BATCH, CONV_W - 1, D_CONV), 1.0),
        'state_pool': nrm(ks[5], (DEPTH, DEC_BATCH, POOL_PAD, D_POOL), 1.0),
        'c_prompt': nrm(ks[6], (BATCH, d), 1.0),
        'c_sample': nrm(ks[7], (DEC_BATCH, d), 1.0),
        'rel_bias': nrm(ks[8], (N_BUCKETS, N_HEADS), 0.5),
        'w_ada': nrm(ks[9], (DEPTH, d, N_MOD * d), 0.5 * d ** -0.5),
        'b_ada': nrm(ks[10], (DEPTH, N_MOD * d), 0.02),
        'norm_g': 1.0 + nrm(ks[11], (DEPTH, 4, d), 0.05),
        'w_in': nrm(ks[12], (DEPTH, d, D_IN), d ** -0.5),
        'sinks': nrm(ks[13], (DEPTH, N_HEADS), 1.0),
        'conv_w': nrm(ks[14], (DEPTH, CONV_W, D_CONV), CONV_W ** -0.5),
        'pool_w': nrm(ks[15], (DEPTH, len(POOL_WINDOWS), POOL_GROUP, POOL_GROUP), POOL_GROUP ** -0.5),
        'pool_scale': 1.0 + nrm(ks[16], (DEPTH, D_POOL), 0.1),
        'w_out': nrm(ks[17], (DEPTH, D_MIX, d), D_MIX ** -0.5),
        'w_gate_dense': nrm(ks[18], (N_DENSE, d, D_FF), d ** -0.5),
        'w_up_dense': nrm(ks[19], (N_DENSE, d, D_FF), d ** -0.5),
        'w_down_dense': nrm(ks[20], (N_DENSE, D_FF, d), D_FF ** -0.5),
        'w_router': nrm(ks[21], (N_MOE, d, N_EXPERTS), d ** -0.5),
        'w_gate_moe': nrm(ks[22], (N_MOE, N_EXPERTS, d, D_FF), d ** -0.5),
        'w_up_moe': nrm(ks[23], (N_MOE, N_EXPERTS, d, D_FF), d ** -0.5),
        'w_down_moe': nrm(ks[24], (N_MOE, N_EXPERTS, D_FF, d), D_FF ** -0.5),
    }


def reference(x_prompt, x_sample, cache_k, cache_v, state_conv, state_pool, c_prompt, c_sample, rel_bias, w_ada, b_ada, norm_g, w_in, sinks, conv_w, pool_w, pool_scale, w_out, w_gate_dense, w_up_dense, w_down_dense, w_router, w_gate_moe, w_up_moe, w_down_moe):
    b = x_prompt.shape[0]
    yp, ys = x_prompt, x_sample
    kp_l, vp_l, cp_l, pp_l = [], [], [], []
    ks_l, vs_l, cs_l, ps_l = [], [], [], []
    for li in range(DEPTH):
        j = li // 2
        if li % 2 == 0:
            ffn = functools.partial(swiglu, wg=w_gate_dense[j], wu=w_up_dense[j], wd=w_down_dense[j])
        else:
            ffn = functools.partial(moe_ffn, w_router=w_router[j], wg=w_gate_moe[j], wu=w_up_moe[j], wd=w_down_moe[j])
        shared = (w_ada[li], b_ada[li], norm_g[li], w_in[li], sinks[li], rel_bias, conv_w[li], pool_w[li], pool_scale[li], w_out[li], ffn)
        zero_conv = jnp.zeros((b, CONV_W - 1, D_CONV), x_prompt.dtype)
        zero_pool = jnp.zeros((b, POOL_PAD, D_POOL), x_prompt.dtype)
        yp, kp, vp, cp, pp = block(yp, c_prompt, 0, None, None, zero_conv, zero_pool, *shared)
        ys, kn, vn, cn, pn = block(ys, c_sample, PAST_LEN, cache_k[li], cache_v[li], state_conv[li], state_pool[li], *shared)
        kp_l.append(kp); vp_l.append(vp); cp_l.append(cp); pp_l.append(pp)
        ks_l.append(kn); vs_l.append(vn); cs_l.append(cn); ps_l.append(pn)
    return (yp, ys, jnp.stack(kp_l), jnp.stack(vp_l), jnp.stack(cp_l), jnp.stack(pp_l), jnp.stack(ks_l), jnp.stack(vs_l), jnp.stack(cs_l), jnp.stack(ps_l))
```

```python
import functools
import math

import jax
import jax.numpy as jnp
from jax import lax
from jax.experimental import pallas as pl
from jax.experimental.pallas import tpu as pltpu

F32 = jnp.float32
BF16 = jnp.bfloat16

N_HEADS = 32
N_KV_HEADS = 4
HEAD_DIM = 64
GQA_GROUP = N_HEADS // N_KV_HEADS
D_ATTN = N_HEADS * HEAD_DIM
D_KV = N_KV_HEADS * HEAD_DIM
WINDOW = 128
N_BUCKETS = 32
MAX_EXACT = N_BUCKETS // 2
MAX_DISTANCE = WINDOW
CONV_W = 3
POOL_WINDOWS = (2, 4, 8, 16)
POOL_PAD = max(POOL_WINDOWS) - 1
N_EXPERTS = 8
TOP_K = 2
N_MOD = 6
RMS_EPS = 1e-6
PAST_LEN = 16384

ROW_BLK = 128
LANES = 128
VMEM_LIMIT = 56 * 1024 * 1024


def _cparams(sem):
    return pltpu.CompilerParams(dimension_semantics=sem, vmem_limit_bytes=VMEM_LIMIT)


def _rms(x, g):
    return x * lax.rsqrt(jnp.mean(x * x, axis=-1, keepdims=True) + RMS_EPS) * g


def _silu(x):
    return x / (1.0 + jnp.exp(-x))


def _ada_kernel(c_ref, w_ref, b_ref, o_ref):
    s = _silu(c_ref[...]).astype(BF16)
    o_ref[...] = jnp.dot(s, w_ref[...].astype(BF16), preferred_element_type=F32) + b_ref[...]


def _ada_mod(c_pad, w_ada, b_ada, tn=512):
    depth, d, n = w_ada.shape
    rows = c_pad.shape[0]
    return pl.pallas_call(
        _ada_kernel,
        grid=(depth, n // tn),
        in_specs=[
            pl.BlockSpec((rows, d), lambda l, j: (0, 0)),
            pl.BlockSpec((None, d, tn), lambda l, j: (l, 0, j)),
            pl.BlockSpec((None, 1, tn), lambda l, j: (l, 0, j)),
        ],
        out_specs=pl.BlockSpec((None, rows, tn), lambda l, j: (l, 0, j)),
        out_shape=jax.ShapeDtypeStruct((depth, rows, n), F32),
        compiler_params=_cparams(("arbitrary", "arbitrary")),
        name="ada_mod",
    )(c_pad, w_ada, b_ada.reshape(depth, 1, n))


def _bias_kernel(rb_ref, bucket_ref, o_ref):
    h = pl.program_id(0)
    bucket = bucket_ref[...]
    acc = jnp.full(bucket.shape, -jnp.inf, F32)
    for b in range(N_BUCKETS):
        acc = jnp.where(bucket == b, rb_ref[b, h], acc)
    o_ref[...] = acc


def _bias_table(rel_bias, bucket):
    rq, rk = bucket.shape
    return pl.pallas_call(
        _bias_kernel,
        grid=(N_HEADS,),
        in_specs=[
            pl.BlockSpec(memory_space=pltpu.SMEM),
            pl.BlockSpec((rq, rk), lambda h: (0, 0)),
        ],
        out_specs=pl.BlockSpec((None, rq, rk), lambda h: (h, 0, 0)),
        out_shape=jax.ShapeDtypeStruct((N_HEADS, rq, rk), F32),
        compiler_params=_cparams(("arbitrary",)),
        name="bias_table",
    )(rel_bias, bucket)


def _t5_bucket(dist):
    n = jnp.maximum(dist, 0)
    nf = jnp.maximum(n, 1).astype(F32)
    large = MAX_EXACT + (jnp.log(nf / MAX_EXACT) / math.log(MAX_DISTANCE / MAX_EXACT)
                         * (N_BUCKETS - MAX_EXACT)).astype(jnp.int32)
    large = jnp.minimum(large, N_BUCKETS - 1)
    return jnp.where(n < MAX_EXACT, n, large)


def _masked_bucket(dist):
    return jnp.where((dist >= 0) & (dist < WINDOW), _t5_bucket(dist), -1).astype(jnp.int32)


def _mod_spec(which, blocks_per_group, n_groups, d):
    last = n_groups - 1
    return pl.BlockSpec((None, ROW_BLK, d),
                        lambda i: (which, jnp.minimum(i // blocks_per_group, last), 0))


def _norm_mod_kernel(x_ref, g_ref, sc_ref, sh_ref, h_ref):
    h = _rms(x_ref[...], g_ref[...]) * (1.0 + sc_ref[...]) + sh_ref[...]
    h_ref[...] = h.astype(BF16)


def _norm_mod(x, norm_g4, layer, mods, blocks_per_group):
    m, d = x.shape
    n_groups = mods.shape[1] // ROW_BLK
    row = pl.BlockSpec((ROW_BLK, d), lambda i: (i, 0))
    return pl.pallas_call(
        _norm_mod_kernel,
        grid=(m // ROW_BLK,),
        in_specs=[
            row,
            pl.BlockSpec((None, None, 1, d), lambda i: (layer, 0, 0, 0)),
            _mod_spec(1, blocks_per_group, n_groups, d),
            _mod_spec(0, blocks_per_group, n_groups, d),
        ],
        out_specs=row,
        out_shape=jax.ShapeDtypeStruct((m, d), BF16),
        compiler_params=_cparams(("arbitrary",)),
        name="norm_mod",
    )(x, norm_g4, mods, mods)


def _top2(logits):
    lane = lax.broadcasted_iota(jnp.int32, logits.shape, 1).astype(F32)
    lg = jnp.where(lane < N_EXPERTS, logits, -jnp.inf)
    m1 = jnp.max(lg, axis=1, keepdims=True)
    i1 = jnp.min(jnp.where(lg == m1, lane, float(LANES)), axis=1, keepdims=True)
    lg2 = jnp.where(lane == i1, -jnp.inf, lg)
    m2 = jnp.max(lg2, axis=1, keepdims=True)
    i2 = jnp.min(jnp.where(lg2 == m2, lane, float(LANES)), axis=1, keepdims=True)
    e2 = jnp.exp(m2 - m1)
    w1 = 1.0 / (1.0 + e2)
    w2 = e2 / (1.0 + e2)
    ridx = jnp.where(lane == 0, i1, jnp.where(lane == 1, i2, 0.0)).astype(jnp.int32)
    rw = jnp.where(lane == 0, w1, jnp.where(lane == 1, w2, 0.0))
    return ridx, rw


def _post_kernel(*refs, has_next, route, gather):
    it = iter(refs)
    if gather:
        pos_ref, rw_ref, ys_hbm = next(it), next(it), next(it)
    else:
        y_ref = next(it)
    x_ref, ga_ref, gate_ref = next(it), next(it), next(it)
    if has_next:
        gb_ref, sc_ref, sh_ref = next(it), next(it), next(it)
    if route:
        wr_ref = next(it)
    xo_ref = next(it)
    if has_next:
        h_ref = next(it)
    if route:
        hf_ref, ridx_ref, rwo_ref = next(it), next(it), next(it)
    if gather:
        buf, sem = next(it), next(it)

    if gather:
        n_rows = TOP_K * ROW_BLK

        def row_copy(r, src_row):
            return pltpu.make_async_copy(ys_hbm.at[pl.ds(src_row, 1), :], buf.at[pl.ds(r, 1), :], sem)

        def issue(r, c):
            row_copy(r, pos_ref[0, r]).start()
            return c

        def drain(r, c):
            row_copy(r, 0).wait()
            return c

        lax.fori_loop(0, n_rows, issue, 0)
        lax.fori_loop(0, n_rows, drain, 0)
        rw = rw_ref[...]
        y = rw[:, 0:1] * buf[0:ROW_BLK, :] + rw[:, 1:2] * buf[ROW_BLK:n_rows, :]
    else:
        y = y_ref[...]

    x = x_ref[...] + gate_ref[...] * _rms(y, ga_ref[...])
    xo_ref[...] = x
    if has_next:
        h = _rms(x, gb_ref[...]) * (1.0 + sc_ref[...]) + sh_ref[...]
        h_ref[...] = h.astype(BF16)
        if route:
            hf_ref[...] = h
            logits = jnp.dot(h.astype(BF16), wr_ref[...].astype(BF16), preferred_element_type=F32)
            ridx, rw_out = _top2(logits)
            ridx_ref[...] = ridx
            rwo_ref[...] = rw_out


def _post(x, y, norm_g4, layer, ga_idx, mods, gate_idx, blocks_per_group, *,
          nxt=None, w_router_pad=None, gather=None):
    m, d = x.shape
    n_groups = mods.shape[1] // ROW_BLK
    row = pl.BlockSpec((ROW_BLK, d), lambda i: (i, 0))
    lane_row = pl.BlockSpec((ROW_BLK, LANES), lambda i: (i, 0))
    has_next = nxt is not None
    route = w_router_pad is not None
    args, specs, scratch = [], [], []
    if gather is not None:
        pos, rw = gather
        args += [pos, rw, y]
        specs += [pl.BlockSpec((None, 1, TOP_K * ROW_BLK), lambda i: (i, 0, 0), memory_space=pltpu.SMEM),
                  lane_row, pl.BlockSpec(memory_space=pl.ANY)]
        scratch = [pltpu.VMEM((TOP_K * ROW_BLK, d), F32), pltpu.SemaphoreType.DMA(())]
    else:
        args.append(y)
        specs.append(row)
    args += [x, norm_g4, mods]
    specs += [row, pl.BlockSpec((None, None, 1, d), lambda i: (layer, ga_idx, 0, 0)),
              _mod_spec(gate_idx, blocks_per_group, n_groups, d)]
    out_shape = [jax.ShapeDtypeStruct((m, d), F32)]
    out_specs = [row]
    if has_next:
        n_layer, n_gidx, n_mods, sc_idx, sh_idx = nxt
        args += [norm_g4, n_mods, n_mods]
        specs += [pl.BlockSpec((None, None, 1, d), lambda i: (n_layer, n_gidx, 0, 0)),
                  _mod_spec(sc_idx, blocks_per_group, n_groups, d),
                  _mod_spec(sh_idx, blocks_per_group, n_groups, d)]
        out_shape.append(jax.ShapeDtypeStruct((m, d), BF16))
        out_specs.append(row)
    if route:
        args.append(w_router_pad)
        specs.append(pl.BlockSpec((d, LANES), lambda i: (0, 0)))
        out_shape += [jax.ShapeDtypeStruct((m, d), F32), jax.ShapeDtypeStruct((m, LANES), jnp.int32),
                      jax.ShapeDtypeStruct((m, LANES), F32)]
        out_specs += [row, lane_row, lane_row]
    return pl.pallas_call(
        functools.partial(_post_kernel, has_next=has_next, route=route, gather=gather is not None),
        grid=(m // ROW_BLK,),
        in_specs=specs,
        out_specs=out_specs,
        out_shape=out_shape,
        scratch_shapes=scratch,
        compiler_params=_cparams(("arbitrary",)),
        name="post_gather" if gather is not None else ("post_route" if route else "post"),
    )(*args)


def _mm_kernel(x_ref, w_ref, o_ref):
    o_ref[...] = jnp.dot(x_ref[...], w_ref[...].astype(BF16), preferred_element_type=F32).astype(o_ref.dtype)


def _matmul(x, w_stack, layer, col_blk0, n_out, tm, tn):
    m, k = x.shape
    return pl.pallas_call(
        _mm_kernel,
        grid=(m // tm, n_out // tn),
        in_specs=[
            pl.BlockSpec((tm, k), lambda i, j: (i, 0)),
            pl.BlockSpec((None, k, tn), lambda i, j: (layer, 0, col_blk0 + j)),
        ],
        out_specs=pl.BlockSpec((tm, tn), lambda i, j: (i, j)),
        out_shape=jax.ShapeDtypeStruct((m, n_out), F32),
        compiler_params=_cparams(("arbitrary", "arbitrary")),
        name="matmul",
    )(x, w_stack)


def _mm2_kernel(a_ref, c_ref, w_ref, o_ref):
    ka = a_ref.shape[1]
    acc = jnp.dot(a_ref[...], w_ref[0:ka, :].astype(BF16), preferred_element_type=F32)
    acc = acc + jnp.dot(c_ref[...], w_ref[ka:, :].astype(BF16), preferred_element_type=F32)
    o_ref[...] = acc


def _matmul2(a, c, w_stack, layer, tm, tn):
    m, ka = a.shape
    kc = c.shape[1]
    n = w_stack.shape[2]
    return pl.pallas_call(
        _mm2_kernel,
        grid=(m // tm, n // tn),
        in_specs=[
            pl.BlockSpec((tm, ka), lambda i, j: (i, 0)),
            pl.BlockSpec((tm, kc), lambda i, j: (i, 0)),
            pl.BlockSpec((None, ka + kc, tn), lambda i, j: (layer, 0, j)),
        ],
        out_specs=pl.BlockSpec((tm, tn), lambda i, j: (i, j)),
        out_shape=jax.ShapeDtypeStruct((m, n), F32),
        compiler_params=_cparams(("arbitrary", "arbitrary")),
        name="matmul_out",
    )(a, c, w_stack)


def _gateup_body(x_ref, wg_ref, wu_ref, o_ref):
    x = x_ref[...]
    g = jnp.dot(x, wg_ref[...].astype(BF16), preferred_element_type=F32)
    u = jnp.dot(x, wu_ref[...].astype(BF16), preferred_element_type=F32)
    o_ref[...] = (_silu(g) * u).astype(BF16)


def _gateup_kernel(x_ref, wg_ref, wu_ref, o_ref):
    _gateup_body(x_ref, wg_ref, wu_ref, o_ref)


def _gateup(x, wg_stack, wu_stack, layer, tm, tn):
    m, k = x.shape
    f = wg_stack.shape[2]
    w_spec = pl.BlockSpec((None, k, tn), lambda i, j: (layer, 0, j))
    return pl.pallas_call(
        _gateup_kernel,
        grid=(m // tm, f // tn),
        in_specs=[pl.BlockSpec((tm, k), lambda i, j: (i, 0)), w_spec, w_spec],
        out_specs=pl.BlockSpec((tm, tn), lambda i, j: (i, j)),
        out_shape=jax.ShapeDtypeStruct((m, f), BF16),
        compiler_params=_cparams(("arbitrary", "arbitrary")),
        name="gateup",
    )(x, wg_stack, wu_stack)


def _down_body(a_ref, w_ref, o_ref, k):
    p = jnp.dot(a_ref[...], w_ref[...].astype(BF16), preferred_element_type=F32)

    @pl.when(k == 0)
    def _():
        o_ref[...] = p

    @pl.when(k > 0)
    def _():
        o_ref[...] += p


def _down_kernel(a_ref, w_ref, o_ref):
    _down_body(a_ref, w_ref, o_ref, pl.program_id(2))


def _down(a, wd_stack, layer, tm, tn, tk):
    m, f = a.shape
    d = wd_stack.shape[2]
    return pl.pallas_call(
        _down_kernel,
        grid=(m // tm, d // tn, f // tk),
        in_specs=[
            pl.BlockSpec((tm, tk), lambda i, j, k: (i, k)),
            pl.BlockSpec((None, tk, tn), lambda i, j, k: (layer, k, j)),
        ],
        out_specs=pl.BlockSpec((tm, tn), lambda i, j, k: (i, j)),
        out_shape=jax.ShapeDtypeStruct((m, d), F32),
        compiler_params=_cparams(("arbitrary", "arbitrary", "arbitrary")),
        name="down",
    )(a, wd_stack)


def _gateup_grouped_kernel(te_ref, nu_ref, x_ref, wg_ref, wu_ref, o_ref):
    used = pl.program_id(0) < nu_ref[0]

    @pl.when(used)
    def _():
        _gateup_body(x_ref, wg_ref, wu_ref, o_ref)

    @pl.when(jnp.logical_not(used))
    def _():
        o_ref[...] = jnp.zeros_like(o_ref)


def _gateup_grouped(xs, wg, wu, layer, tile_e, n_used, tm, tn):
    r, k = xs.shape
    f = wg.shape[3]
    nj = f // tn

    def w_map(i, j, te, nu):
        return (layer, te[i], 0, jnp.where(i < nu[0], j, nj - 1))

    w_spec = pl.BlockSpec((None, None, k, tn), w_map)
    return pl.pallas_call(
        _gateup_grouped_kernel,
        grid_spec=pltpu.PrefetchScalarGridSpec(
            num_scalar_prefetch=2,
            grid=(r // tm, nj),
            in_specs=[pl.BlockSpec((tm, k), lambda i, j, te, nu: (jnp.minimum(i, nu[0] - 1), 0)), w_spec, w_spec],
            out_specs=pl.BlockSpec((tm, tn), lambda i, j, te, nu: (i, j)),
        ),
        out_shape=jax.ShapeDtypeStruct((r, f), BF16),
        compiler_params=_cparams(("arbitrary", "arbitrary")),
        name="gateup_grouped",
    )(tile_e, n_used, xs, wg, wu)


def _down_grouped_kernel(te_ref, nu_ref, a_ref, w_ref, o_ref):
    used = pl.program_id(0) < nu_ref[0]

    @pl.when(used)
    def _():
        _down_body(a_ref, w_ref, o_ref, pl.program_id(2))

    @pl.when(jnp.logical_not(used))
    def _():
        o_ref[...] = jnp.zeros_like(o_ref)


def _down_grouped(a, wd, layer, tile_e, n_used, tm, tn, tk):
    r, f = a.shape
    d = wd.shape[3]
    nk = f // tk
    nj = d // tn

    def a_map(i, j, k, te, nu):
        return (jnp.minimum(i, nu[0] - 1), jnp.where(i < nu[0], k, nk - 1))

    def w_map(i, j, k, te, nu):
        used = i < nu[0]
        return (layer, te[i], jnp.where(used, k, nk - 1), jnp.where(used, j, nj - 1))

    return pl.pallas_call(
        _down_grouped_kernel,
        grid_spec=pltpu.PrefetchScalarGridSpec(
            num_scalar_prefetch=2,
            grid=(r // tm, nj, nk),
            in_specs=[pl.BlockSpec((tm, tk), a_map), pl.BlockSpec((None, None, tk, tn), w_map)],
            out_specs=pl.BlockSpec((tm, tn), lambda i, j, k, te, nu: (i, j)),
        ),
        out_shape=jax.ShapeDtypeStruct((r, d), F32),
        compiler_params=_cparams(("arbitrary", "arbitrary", "arbitrary")),
        name="down_grouped",
    )(tile_e, n_used, a, wd)


def _dispatch_kernel(nr_ref, src_ref, h_hbm, o_ref, buf, sem):
    rows = o_ref.shape[0]
    used = pl.program_id(0) * rows < nr_ref[0]

    def row_copy(r, src_row):
        return pltpu.make_async_copy(h_hbm.at[pl.ds(src_row, 1), :], buf.at[pl.ds(r, 1), :], sem)

    @pl.when(used)
    def _():
        def issue(r, c):
            row_copy(r, src_ref[0, r]).start()
            return c

        def drain(r, c):
            row_copy(r, 0).wait()
            return c

        lax.fori_loop(0, rows, issue, 0)
        lax.fori_loop(0, rows, drain, 0)
        o_ref[...] = buf[...].astype(BF16)

    @pl.when(jnp.logical_not(used))
    def _():
        o_ref[...] = jnp.zeros_like(o_ref)


def _dispatch(h_f32, src, n_used_rows, rows_per_step):
    d = h_f32.shape[1]
    r = src.shape[0]
    steps = r // rows_per_step
    return pl.pallas_call(
        _dispatch_kernel,
        grid_spec=pltpu.PrefetchScalarGridSpec(
            num_scalar_prefetch=1,
            grid=(steps,),
            in_specs=[
                pl.BlockSpec((None, 1, rows_per_step), lambda i, nr: (i, 0, 0), memory_space=pltpu.SMEM),
                pl.BlockSpec(memory_space=pl.ANY),
            ],
            out_specs=pl.BlockSpec((rows_per_step, d), lambda i, nr: (i, 0)),
            scratch_shapes=[pltpu.VMEM((rows_per_step, d), F32), pltpu.SemaphoreType.DMA(())],
        ),
        out_shape=jax.ShapeDtypeStruct((r, d), BF16),
        compiler_params=_cparams(("arbitrary",)),
        name="dispatch",
    )(n_used_rows, src.reshape(steps, 1, rows_per_step), h_f32)


def _route_meta(ridx, tm, n_tiles):
    m = ridx.shape[0]
    e_flat = ridx[:, :TOP_K].reshape(-1)
    onehot = (e_flat[:, None] == jnp.arange(N_EXPERTS, dtype=jnp.int32)[None, :]).astype(jnp.int32)
    csum = jnp.cumsum(onehot, axis=0)
    rank = jnp.take_along_axis(csum, e_flat[:, None], axis=1)[:, 0] - 1
    counts = csum[-1]
    ptiles = (counts + tm - 1) // tm
    tile_end = jnp.cumsum(ptiles)
    tile_start = tile_end - ptiles
    pos = (tile_start[e_flat] * tm + rank).astype(jnp.int32)
    n_used = tile_end[-1].astype(jnp.int32)
    tiles = jnp.arange(n_tiles, dtype=jnp.int32)
    tile_e_raw = jnp.sum((tiles[:, None] >= tile_end[None, :]).astype(jnp.int32), axis=1)
    last_e = tile_e_raw[jnp.maximum(n_used - 1, 0)]
    tile_e = jnp.minimum(jnp.where(tiles < n_used, tile_e_raw, last_e), N_EXPERTS - 1).astype(jnp.int32)
    src = jnp.zeros((n_tiles * tm,), jnp.int32).at[pos].set(jnp.arange(m * TOP_K, dtype=jnp.int32) // TOP_K)
    return pos.reshape(m, TOP_K), src, tile_e, n_used.reshape(1)


def _softmax_sink_pv(s, sink, vv):
    m = jnp.maximum(jnp.max(s, axis=1, keepdims=True), sink)
    p = jnp.exp(s - m)
    den = jnp.sum(p, axis=1, keepdims=True) + jnp.exp(sink - m)
    return jnp.dot((p / den).astype(BF16), vv, preferred_element_type=F32)


def _qk(q, kk):
    return lax.dot_general(q, kk, (((1,), (1,)), ((), ())), preferred_element_type=F32) * (1.0 / math.sqrt(HEAD_DIM))


def _attn_prompt_kernel(sink_ref, q_ref, kp_ref, kc_ref, vp_ref, vc_ref, bias_ref, tail_ref, o_ref, *,
                        layer, blocks_per_seq, n_blocks):
    i = pl.program_id(0)

    @pl.when(i < n_blocks)
    def _():
        first = (i % blocks_per_seq) == 0
        col = lax.broadcasted_iota(jnp.int32, (ROW_BLK, 2 * ROW_BLK), 1)
        no_prev = jnp.logical_and(first, col < ROW_BLK)
        for kvh in range(N_KV_HEADS):
            cs = slice(kvh * HEAD_DIM, (kvh + 1) * HEAD_DIM)
            kk = jnp.concatenate([kp_ref[:, cs], kc_ref[:, cs]], axis=0).astype(BF16)
            vv = jnp.concatenate([vp_ref[:, cs], vc_ref[:, cs]], axis=0).astype(BF16)
            for g in range(GQA_GROUP):
                h = kvh * GQA_GROUP + g
                hs = slice(h * HEAD_DIM, (h + 1) * HEAD_DIM)
                s = _qk(q_ref[:, hs].astype(BF16), kk) + bias_ref[h]
                s = jnp.where(no_prev, -jnp.inf, s)
                o_ref[:, hs] = _softmax_sink_pv(s, sink_ref[layer, h], vv).astype(BF16)

    @pl.when(i >= n_blocks)
    def _():
        o_ref[...] = tail_ref[...]


def _attn_prompt(qkv, sinks, bias_p, tail, layer, n_prompt_rows, blocks_per_seq):
    kb = D_ATTN // D_KV
    n_blocks = n_prompt_rows // ROW_BLK

    def prev(i):
        return jnp.maximum(i - 1, 0)

    return pl.pallas_call(
        functools.partial(_attn_prompt_kernel, layer=layer, blocks_per_seq=blocks_per_seq, n_blocks=n_blocks),
        grid=(n_blocks + 1,),
        in_specs=[
            pl.BlockSpec(memory_space=pltpu.SMEM),
            pl.BlockSpec((ROW_BLK, D_ATTN), lambda i: (i, 0)),
            pl.BlockSpec((ROW_BLK, D_KV), lambda i: (prev(i), kb)),
            pl.BlockSpec((ROW_BLK, D_KV), lambda i: (i, kb)),
            pl.BlockSpec((ROW_BLK, D_KV), lambda i: (prev(i), kb + 1)),
            pl.BlockSpec((ROW_BLK, D_KV), lambda i: (i, kb + 1)),
            pl.BlockSpec((N_HEADS, ROW_BLK, 2 * ROW_BLK), lambda i: (0, 0, 0)),
            pl.BlockSpec((ROW_BLK, D_ATTN), lambda i: (0, 0)),
        ],
        out_specs=pl.BlockSpec((ROW_BLK, D_ATTN), lambda i: (i, 0)),
        out_shape=jax.ShapeDtypeStruct((n_prompt_rows + ROW_BLK, D_ATTN), BF16),
        compiler_params=_cparams(("arbitrary",)),
        name="attn_prompt",
    )(sinks, qkv, qkv, qkv, qkv, qkv, bias_p, tail)


def _attn_sample_kernel(q_ref, ck_ref, cv_ref, kn_ref, vn_ref, bias_ref, sink_ref, o_ref):
    for kvh in range(N_KV_HEADS):
        kk = jnp.concatenate([ck_ref[kvh], kn_ref[kvh]], axis=0).astype(BF16)
        vv = jnp.concatenate([cv_ref[kvh], vn_ref[kvh]], axis=0).astype(BF16)
        s = _qk(q_ref[kvh].astype(BF16), kk) + bias_ref[kvh]
        o_ref[kvh] = _softmax_sink_pv(s, sink_ref[kvh], vv)


def _attn_sample(q, ck, cv, kn, vn, bias_s, sink_s):
    b, _, rows, _ = q.shape
    wb = ck.shape[2]
    pad = kn.shape[2]

    def per_b(n):
        return pl.BlockSpec((None, N_KV_HEADS, n, HEAD_DIM), lambda i: (i, 0, 0, 0))

    return pl.pallas_call(
        _attn_sample_kernel,
        grid=(b,),
        in_specs=[
            per_b(rows), per_b(wb), per_b(wb), per_b(pad), per_b(pad),
            pl.BlockSpec((N_KV_HEADS, rows, wb + pad), lambda i: (0, 0, 0)),
            pl.BlockSpec((N_KV_HEADS, rows, 1), lambda i: (0, 0, 0)),
        ],
        out_specs=per_b(rows),
        out_shape=jax.ShapeDtypeStruct(q.shape, F32),
        compiler_params=_cparams(("arbitrary",)),
        name="attn_sample",
    )(q, ck, cv, kn, vn, bias_s, sink_s)


def _conv_pool(ue_ref, ve_ref, conv_base, pool_base, stride, rows, gate_b, pv, cw_ref, pw_ref, ps_ref, cnt_of, o_ref):
    d_conv = gate_b.shape[1]
    y = cw_ref[CONV_W - 1:CONV_W, :] * ue_ref[conv_base:conv_base + rows, :]
    for j in range(1, CONV_W):
        off = conv_base - j * stride
        y = y + cw_ref[CONV_W - 1 - j:CONV_W - j, :] * ue_ref[off:off + rows, :]
    o_ref[:, 0:d_conv] = (gate_b * y).astype(BF16)
    grp = pv.shape[1] // len(POOL_WINDOWS)
    for gi, w in enumerate(POOL_WINDOWS):
        cs = slice(gi * grp, (gi + 1) * grp)
        acc = ve_ref[pool_base:pool_base + rows, cs]
        for j in range(1, w):
            off = pool_base - j * stride
            acc = acc + ve_ref[off:off + rows, cs]
        g = acc / cnt_of(w) - pv[:, cs]
        yg = jnp.dot(g.astype(BF16), pw_ref[gi].astype(BF16), preferred_element_type=F32) * ps_ref[:, cs]
        o_ref[:, d_conv + gi * grp:d_conv + (gi + 1) * grp] = yg.astype(BF16)


CONV_HALO = 8
POOL_HALO = 16


def _convpool_prompt_kernel(gb_ref, gc_ref, xt_ref, pv_ref, gch_ref, xth_ref, pvh_ref, cw_ref, pw_ref, ps_ref,
                            tail_ref, o_ref, ut_ref, ue_ref, ve_ref, *, blocks_per_seq, n_blocks):
    i = pl.program_id(0)

    @pl.when(i < n_blocks)
    def _():
        n = i % blocks_per_seq
        keep = (n != 0).astype(F32)
        u = gc_ref[...] * xt_ref[...]
        ue_ref[0:CONV_HALO, :] = gch_ref[...] * xth_ref[...] * keep
        ue_ref[CONV_HALO:CONV_HALO + ROW_BLK, :] = u
        ut_ref[...] = u[ROW_BLK - CONV_HALO:ROW_BLK, :]
        pv = pv_ref[...]
        ve_ref[0:POOL_HALO, :] = pvh_ref[...] * keep
        ve_ref[POOL_HALO:POOL_HALO + ROW_BLK, :] = pv
        pos1 = (n * ROW_BLK + 1 + lax.broadcasted_iota(jnp.int32, (ROW_BLK, 1), 0)).astype(F32)

        def cnt_of(w):
            return jnp.minimum(pos1, float(w))

        _conv_pool(ue_ref, ve_ref, CONV_HALO, POOL_HALO, 1, ROW_BLK, gb_ref[...], pv, cw_ref, pw_ref, ps_ref,
                   cnt_of, o_ref)

    @pl.when(i >= n_blocks)
    def _():
        o_ref[...] = tail_ref[...]


def _convpool_prompt(rest, tail, conv_w, pool_w, pool_scale3, layer, n_prompt_rows, blocks_per_seq):
    c = rest.shape[1] // 4
    n_blocks = n_prompt_rows // ROW_BLK
    n_seq = n_blocks // blocks_per_seq
    ch, ph = ROW_BLK // CONV_HALO, ROW_BLK // POOL_HALO

    def blk(col):
        return pl.BlockSpec((ROW_BLK, c), lambda i: (i, col))

    def halo(rows, per_blk, col):
        return pl.BlockSpec((rows, c), lambda i: (jnp.maximum(i * per_blk - 1, 0), col))

    return pl.pallas_call(
        functools.partial(_convpool_prompt_kernel, blocks_per_seq=blocks_per_seq, n_blocks=n_blocks),
        grid=(n_blocks + 1,),
        in_specs=[
            blk(0), blk(1), blk(2), blk(3),
            halo(CONV_HALO, ch, 1), halo(CONV_HALO, ch, 2), halo(POOL_HALO, ph, 3),
            pl.BlockSpec((None, CONV_W, c), lambda i: (layer, 0, 0)),
            pl.BlockSpec((None, len(POOL_WINDOWS), c // 4, c // 4), lambda i: (layer, 0, 0, 0)),
            pl.BlockSpec((None, 1, c), lambda i: (layer, 0, 0)),
            pl.BlockSpec((ROW_BLK, 2 * c), lambda i: (0, 0)),
        ],
        out_specs=[
            pl.BlockSpec((ROW_BLK, 2 * c), lambda i: (i, 0)),
            pl.BlockSpec((CONV_HALO, c), lambda i: (jnp.minimum(i // blocks_per_seq, n_seq - 1), 0)),
        ],
        out_shape=[jax.ShapeDtypeStruct((n_prompt_rows + ROW_BLK, 2 * c), BF16),
                   jax.ShapeDtypeStruct((n_seq * CONV_HALO, c), F32)],
        scratch_shapes=[pltpu.VMEM((CONV_HALO + ROW_BLK, c), F32), pltpu.VMEM((POOL_HALO + ROW_BLK, c), F32)],
        compiler_params=_cparams(("arbitrary",)),
        name="convpool_prompt",
    )(rest, rest, rest, rest, rest, rest, rest, conv_w, pool_w, pool_scale3, tail)


def _convpool_sample_kernel(gb_ref, gc_ref, xt_ref, pv_ref, cprev_ref, pprev_ref, cw_ref, pw_ref, ps_ref,
                            o_ref, u_ref, ue_ref, ve_ref, *, stride, pos0):
    rows = gb_ref.shape[0]
    ch, ph = cprev_ref.shape[0], pprev_ref.shape[0]
    u = gc_ref[...] * xt_ref[...]
    u_ref[...] = u
    ue_ref[0:ch, :] = cprev_ref[...]
    ue_ref[ch:ch + rows, :] = u
    pv = pv_ref[...]
    ve_ref[0:ph, :] = pprev_ref[...]
    ve_ref[ph:ph + rows, :] = pv
    pos1 = (pos0 + 1 + lax.broadcasted_iota(jnp.int32, (rows, 1), 0) // stride).astype(F32)

    def cnt_of(w):
        return jnp.minimum(pos1, float(w))

    _conv_pool(ue_ref, ve_ref, ch, ph, stride, rows, gb_ref[...], pv, cw_ref, pw_ref, ps_ref, cnt_of, o_ref)


def _convpool_sample(rest, row_blk_idx, cprev, pprev, conv_w, pool_w, pool_scale3, layer, stride, pos0):
    c = rest.shape[1] // 4
    rows = ROW_BLK

    def blk(col):
        return pl.BlockSpec((rows, c), lambda i: (row_blk_idx, col))

    return pl.pallas_call(
        functools.partial(_convpool_sample_kernel, stride=stride, pos0=pos0),
        grid=(1,),
        in_specs=[
            blk(0), blk(1), blk(2), blk(3),
            pl.BlockSpec(cprev.shape, lambda i: (0, 0)),
            pl.BlockSpec(pprev.shape, lambda i: (0, 0)),
            pl.BlockSpec((None, CONV_W, c), lambda i: (layer, 0, 0)),
            pl.BlockSpec((None, len(POOL_WINDOWS), c // 4, c // 4), lambda i: (layer, 0, 0, 0)),
            pl.BlockSpec((None, 1, c), lambda i: (layer, 0, 0)),
        ],
        out_specs=[pl.BlockSpec((rows, 2 * c), lambda i: (0, 0)), pl.BlockSpec((rows, c), lambda i: (0, 0))],
        out_shape=[jax.ShapeDtypeStruct((rows, 2 * c), BF16), jax.ShapeDtypeStruct((rows, c), F32)],
        scratch_shapes=[pltpu.VMEM((cprev.shape[0] + rows, c), F32), pltpu.VMEM((pprev.shape[0] + rows, c), F32)],
        compiler_params=_cparams(("arbitrary",)),
        name="convpool_sample",
    )(rest, rest, rest, rest, cprev, pprev, conv_w, pool_w, pool_scale3)


TM = 1040
TM_MOE = 1040
TN_PROJ = 512
TN_FF = 256
TN_DOWN = 2048
TK_DOWN = 512
DISPATCH_ROWS = 208
KN_PAD = 16


def kernel(x_prompt, x_sample, cache_k, cache_v, state_conv, state_pool, c_prompt, c_sample, rel_bias, w_ada, b_ada, norm_g, w_in, sinks, conv_w, pool_w, pool_scale, w_out, w_gate_dense, w_up_dense, w_down_dense, w_router, w_gate_moe, w_up_moe, w_down_moe):
    bp, t, d = x_prompt.shape
    bs, s, _ = x_sample.shape
    depth = w_ada.shape[0]
    wb = cache_k.shape[2]
    n_p = bp * t
    n_s = bs * s
    m = n_p + n_s
    assert n_s == ROW_BLK and t % ROW_BLK == 0 and m % TM == 0
    blocks_per_seq = t // ROW_BLK
    d_qkv = D_ATTN + 2 * D_KV
    c_ch = (w_in.shape[2] - d_qkv) // 4

    x = jnp.concatenate([x_prompt.reshape(n_p, d), x_sample.transpose(1, 0, 2).reshape(n_s, d)], axis=0)

    n_c = bp + bs
    c_rows = -(-n_c // 16) * 16
    c_all = jnp.concatenate([c_prompt, c_sample, jnp.zeros((c_rows - n_c, d), F32)], axis=0)
    mod = _ada_mod(c_all, w_ada, b_ada)

    def mod_rows(l):
        ml = mod[l].reshape(c_rows, N_MOD, d).transpose(1, 0, 2)
        mp = jnp.broadcast_to(ml[:, :bp, None, :], (N_MOD, bp, ROW_BLK, d)).reshape(N_MOD, bp * ROW_BLK, d)
        ms = jnp.tile(ml[:, bp:bp + bs], (1, s, 1))
        return jnp.concatenate([mp, ms], axis=1)

    mods = [mod_rows(l) for l in range(depth)]
    norm_g4 = norm_g.reshape(depth, 4, 1, d)

    qi = jnp.arange(ROW_BLK)[:, None]
    dist_p = ROW_BLK + qi - jnp.arange(2 * ROW_BLK)[None, :]
    bias_p = _bias_table(rel_bias, _masked_bucket(dist_p))
    kn_pad = KN_PAD
    si = jnp.arange(8)[:, None]
    dist_s = si + wb - jnp.arange(wb + kn_pad)[None, :]
    bucket_s = jnp.where((si < s) & (jnp.arange(wb + kn_pad)[None, :] < wb + s), _masked_bucket(dist_s), -1)
    bias_s = _bias_table(rel_bias, bucket_s)[:, :s]
    bias_s = bias_s.reshape(N_KV_HEADS, GQA_GROUP, s, wb + kn_pad).transpose(0, 2, 1, 3)
    bias_s = bias_s.reshape(N_KV_HEADS, s * GQA_GROUP, wb + kn_pad)

    pool_scale3 = pool_scale.reshape(depth, 1, c_ch)
    w_router_pad = jnp.pad(w_router, ((0, 0), (0, 0), (0, LANES - N_EXPERTS)))

    h = _norm_mod(x, norm_g4, 0, mods[0], blocks_per_seq)
    kp_l, vp_l, cp_l, pp_l, ks_l, vs_l, cs_l, ps_l = ([] for _ in range(8))
    for l in range(depth):
        j = l // 2
        is_moe = l % 2 == 1
        qkv = _matmul(h, w_in, l, 0, d_qkv, TM, TN_PROJ)
        rest = _matmul(h, w_in, l, d_qkv // TN_PROJ, 4 * c_ch, TM, TN_PROJ)

        qs = qkv[n_p:, :D_ATTN].reshape(s, bs, N_KV_HEADS, GQA_GROUP, HEAD_DIM)
        qs = qs.transpose(1, 2, 0, 3, 4).reshape(bs, N_KV_HEADS, s * GQA_GROUP, HEAD_DIM)
        k_new = qkv[n_p:, D_ATTN:D_ATTN + D_KV].reshape(s, bs, N_KV_HEADS, HEAD_DIM).transpose(1, 0, 2, 3)
        v_new = qkv[n_p:, D_ATTN + D_KV:].reshape(s, bs, N_KV_HEADS, HEAD_DIM).transpose(1, 0, 2, 3)
        pad_new = lambda a: jnp.pad(a.transpose(0, 2, 1, 3), ((0, 0), (0, 0), (0, kn_pad - s), (0, 0)))
        sink_s = jnp.tile(sinks[l].reshape(N_KV_HEADS, 1, GQA_GROUP), (1, s, 1)).reshape(N_KV_HEADS, s * GQA_GROUP, 1)
        o_s = _attn_sample(qs, cache_k[l].transpose(0, 2, 1, 3), cache_v[l].transpose(0, 2, 1, 3),
                           pad_new(k_new), pad_new(v_new), bias_s, sink_s)
        o_s = o_s.reshape(bs, N_KV_HEADS, s, GQA_GROUP, HEAD_DIM).transpose(2, 0, 1, 3, 4).reshape(n_s, D_ATTN)
        a_all = _attn_prompt(qkv, sinks, bias_p, o_s.astype(BF16), l, n_p, blocks_per_seq)

        cprev = state_conv[l].transpose(1, 0, 2).reshape((CONV_W - 1) * bs, c_ch)
        pprev = state_pool[l].transpose(1, 0, 2).reshape(POOL_PAD * bs, c_ch)
        cp_s, u_s = _convpool_sample(rest, n_p // ROW_BLK, cprev, pprev, conv_w, pool_w, pool_scale3, l, bs, PAST_LEN)
        cp_all, u_tail = _convpool_prompt(rest, cp_s, conv_w, pool_w, pool_scale3, l, n_p, blocks_per_seq)

        kv_p = qkv[:n_p].reshape(bp, t, d_qkv)[:, t - WINDOW:]
        kp_l.append(kv_p[..., D_ATTN:D_ATTN + D_KV].reshape(bp, WINDOW, N_KV_HEADS, HEAD_DIM))
        vp_l.append(kv_p[..., D_ATTN + D_KV:].reshape(bp, WINDOW, N_KV_HEADS, HEAD_DIM))
        cp_l.append(u_tail.reshape(bp, CONV_HALO, c_ch)[:, CONV_HALO - (CONV_W - 1):])
        pp_l.append(rest[:n_p].reshape(bp, t, 4 * c_ch)[:, t - POOL_PAD:, 3 * c_ch:])
        ks_l.append(jnp.concatenate([cache_k[l], k_new], axis=1)[:, s:])
        vs_l.append(jnp.concatenate([cache_v[l], v_new], axis=1)[:, s:])
        u_bs = u_s.reshape(s, bs, c_ch).transpose(1, 0, 2)
        cs_l.append(jnp.concatenate([state_conv[l], u_bs], axis=1)[:, s:])
        pv_bs = rest[n_p:, 3 * c_ch:].reshape(s, bs, c_ch).transpose(1, 0, 2)
        ps_l.append(jnp.concatenate([state_pool[l], pv_bs], axis=1)[:, s:])

        mo = _matmul2(a_all, cp_all, w_out, l, TM, TN_PROJ)
        nxt2 = (l, 2, mods[l], 4, 3)
        last = l == depth - 1
        nxt_layer = None if last else (l + 1, 0, mods[l + 1], 1, 0)
        if not is_moe:
            x, h2 = _post(x, mo, norm_g4, l, 1, mods[l], 2, blocks_per_seq, nxt=nxt2)
            act = _gateup(h2, w_gate_dense, w_up_dense, j, TM, TN_FF)
            y = _down(act, w_down_dense, j, TM, TN_DOWN, TK_DOWN)
            outs = _post(x, y, norm_g4, l, 3, mods[l], 5, blocks_per_seq, nxt=nxt_layer)
        else:
            x, h2, h2f, ridx, rw = _post(x, mo, norm_g4, l, 1, mods[l], 2, blocks_per_seq, nxt=nxt2,
                                          w_router_pad=w_router_pad[j])
            n_tiles = (m * TOP_K) // TM_MOE + N_EXPERTS
            pos, src, tile_e, n_used = _route_meta(ridx, TM_MOE, n_tiles)
            hs = _dispatch(h2f, src, n_used * TM_MOE, DISPATCH_ROWS)
            act = _gateup_grouped(hs, w_gate_moe, w_up_moe, j, tile_e, n_used, TM_MOE, TN_FF)
            ys = _down_grouped(act, w_down_moe, j, tile_e, n_used, TM_MOE, TN_DOWN, TK_DOWN)
            pos_blk = pos.reshape(m // ROW_BLK, ROW_BLK, TOP_K).transpose(0, 2, 1).reshape(m // ROW_BLK, 1, TOP_K * ROW_BLK)
            outs = _post(x, ys, norm_g4, l, 3, mods[l], 5, blocks_per_seq, nxt=nxt_layer, gather=(pos_blk, rw))
        if last:
            (x,) = outs
        else:
            x, h = outs

    y_prompt = x[:n_p].reshape(bp, t, d)
    y_sample = x[n_p:].reshape(s, bs, d).transpose(1, 0, 2)
    st = jnp.stack
    return (y_prompt, y_sample, st(kp_l), st(vp_l), st(cp_l), st(pp_l), st(ks_l), st(vs_l), st(cs_l), st(ps_l))
```

```python
import functools
import math

import jax
import jax.numpy as jnp
from jax import lax
from jax.experimental import pallas as pl
from jax.experimental.pallas import tpu as pltpu

F32 = jnp.float32
BF16 = jnp.bfloat16

N_HEADS = 32
N_KV_HEADS = 4
HEAD_DIM = 64
GQA_GROUP = N_HEADS // N_KV_HEADS
D_ATTN = N_HEADS * HEAD_DIM
D_KV = N_KV_HEADS * HEAD_DIM
WINDOW = 128
N_BUCKETS = 32
MAX_EXACT = N_BUCKETS // 2
MAX_DISTANCE = WINDOW
CONV_W = 3
POOL_WINDOWS = (2, 4, 8, 16)
POOL_PAD = max(POOL_WINDOWS) - 1
N_EXPERTS = 8
TOP_K = 2
N_MOD = 6
RMS_EPS = 1e-6
PAST_LEN = 16384

ROW_BLK = 128
LANES = 128
VMEM_LIMIT = 56 * 1024 * 1024


def _cparams(sem):
    return pltpu.CompilerParams(dimension_semantics=sem, vmem_limit_bytes=VMEM_LIMIT)


def _rms(x, g):
    return x * lax.rsqrt(jnp.mean(x * x, axis=-1, keepdims=True) + RMS_EPS) * g


def _silu(x):
    return x / (1.0 + jnp.exp(-x))


def _ada_kernel(c_ref, w_ref, b_ref, o_ref):
    s = _silu(c_ref[...]).astype(BF16)
    o_ref[...] = jnp.dot(s, w_ref[...].astype(BF16), preferred_element_type=F32) + b_ref[...]


def _ada_mod(c_pad, w_ada, b_ada, tn=512):
    depth, d, n = w_ada.shape
    rows = c_pad.shape[0]
    return pl.pallas_call(
        _ada_kernel,
        grid=(depth, n // tn),
        in_specs=[
            pl.BlockSpec((rows, d), lambda l, j: (0, 0)),
            pl.BlockSpec((None, d, tn), lambda l, j: (l, 0, j)),
            pl.BlockSpec((None, 1, tn), lambda l, j: (l, 0, j)),
        ],
        out_specs=pl.BlockSpec((None, rows, tn), lambda l, j: (l, 0, j)),
        out_shape=jax.ShapeDtypeStruct((depth, rows, n), F32),
        compiler_params=_cparams(("arbitrary", "arbitrary")),
        name="ada_mod",
    )(c_pad, w_ada, b_ada.reshape(depth, 1, n))


def _bias_kernel(rb_ref, bucket_ref, o_ref):
    h = pl.program_id(0)
    bucket = bucket_ref[...]
    acc = jnp.full(bucket.shape, -jnp.inf, F32)
    for b in range(N_BUCKETS):
        acc = jnp.where(bucket == b, rb_ref[b, h], acc)
    o_ref[...] = acc


def _bias_table(rel_bias, bucket):
    rq, rk = bucket.shape
    return pl.pallas_call(
        _bias_kernel,
        grid=(N_HEADS,),
        in_specs=[
            pl.BlockSpec(memory_space=pltpu.SMEM),
            pl.BlockSpec((rq, rk), lambda h: (0, 0)),
        ],
        out_specs=pl.BlockSpec((None, rq, rk), lambda h: (h, 0, 0)),
        out_shape=jax.ShapeDtypeStruct((N_HEADS, rq, rk), F32),
        compiler_params=_cparams(("arbitrary",)),
        name="bias_table",
    )(rel_bias, bucket)


def _t5_bucket(dist):
    n = jnp.maximum(dist, 0)
    nf = jnp.maximum(n, 1).astype(F32)
    large = MAX_EXACT + (jnp.log(nf / MAX_EXACT) / math.log(MAX_DISTANCE / MAX_EXACT)
                         * (N_BUCKETS - MAX_EXACT)).astype(jnp.int32)
    large = jnp.minimum(large, N_BUCKETS - 1)
    return jnp.where(n < MAX_EXACT, n, large)


def _masked_bucket(dist):
    return jnp.where((dist >= 0) & (dist < WINDOW), _t5_bucket(dist), -1).astype(jnp.int32)


def _mod_spec(which, blocks_per_group, n_groups, d):
    last = n_groups - 1
    return pl.BlockSpec((None, ROW_BLK, d),
                        lambda i: (which, jnp.minimum(i // blocks_per_group, last), 0))


def _split_specs(n_lead_blocks, d):
    lead = pl.BlockSpec((ROW_BLK, d), lambda i: (jnp.minimum(i, n_lead_blocks - 1), 0))
    tail = pl.BlockSpec((ROW_BLK, d), lambda i: (0, 0))
    return [lead, tail]


def _read_split(lead_ref, tail_ref, n_lead_blocks):
    return jnp.where(pl.program_id(0) < n_lead_blocks, lead_ref[...], tail_ref[...])


def _norm_mod_kernel(xp_ref, xs_ref, g_ref, sc_ref, sh_ref, h_ref, *, n_lead_blocks):
    x = _read_split(xp_ref, xs_ref, n_lead_blocks)
    h = _rms(x, g_ref[...]) * (1.0 + sc_ref[...]) + sh_ref[...]
    h_ref[...] = h.astype(BF16)


def _norm_mod(x_lead, x_tail, norm_g4, layer, mods, blocks_per_group):
    n_lead, d = x_lead.shape
    m = n_lead + x_tail.shape[0]
    n_groups = mods.shape[1] // ROW_BLK
    return pl.pallas_call(
        functools.partial(_norm_mod_kernel, n_lead_blocks=n_lead // ROW_BLK),
        grid=(m // ROW_BLK,),
        in_specs=_split_specs(n_lead // ROW_BLK, d) + [
            pl.BlockSpec((None, None, 1, d), lambda i: (layer, 0, 0, 0)),
            _mod_spec(1, blocks_per_group, n_groups, d),
            _mod_spec(0, blocks_per_group, n_groups, d),
        ],
        out_specs=pl.BlockSpec((ROW_BLK, d), lambda i: (i, 0)),
        out_shape=jax.ShapeDtypeStruct((m, d), BF16),
        compiler_params=_cparams(("arbitrary",)),
        name="norm_mod",
    )(x_lead, x_tail, norm_g4, mods, mods)


def _top2(logits):
    lane = lax.broadcasted_iota(jnp.int32, logits.shape, 1).astype(F32)
    lg = jnp.where(lane < N_EXPERTS, logits, -jnp.inf)
    m1 = jnp.max(lg, axis=1, keepdims=True)
    i1 = jnp.min(jnp.where(lg == m1, lane, float(LANES)), axis=1, keepdims=True)
    lg2 = jnp.where(lane == i1, -jnp.inf, lg)
    m2 = jnp.max(lg2, axis=1, keepdims=True)
    i2 = jnp.min(jnp.where(lg2 == m2, lane, float(LANES)), axis=1, keepdims=True)
    e2 = jnp.exp(m2 - m1)
    w1 = 1.0 / (1.0 + e2)
    w2 = e2 / (1.0 + e2)
    ridx = jnp.where(lane == 0, i1, jnp.where(lane == 1, i2, 0.0)).astype(jnp.int32)
    rw = jnp.where(lane == 0, w1, jnp.where(lane == 1, w2, 0.0))
    return ridx, rw


def _post_kernel(*refs, has_next, route, gather, split_in, split_out):
    it = iter(refs)
    if gather:
        pos_ref, rw_ref, ys_hbm = next(it), next(it), next(it)
    else:
        y_ref = next(it)
    x_ref = next(it)
    if split_in:
        xs_ref = next(it)
    ga_ref, gate_ref = next(it), next(it)
    if has_next:
        gb_ref, sc_ref, sh_ref = next(it), next(it), next(it)
    if route:
        wr_ref = next(it)
    xo_ref = next(it)
    if split_out:
        xos_ref = next(it)
    if has_next:
        h_ref = next(it)
    if route:
        hf_ref, ridx_ref, rwo_ref = next(it), next(it), next(it)
    if gather:
        buf, sem = next(it), next(it)

    if gather:
        n_rows = TOP_K * ROW_BLK

        def row_copy(r, src_row):
            return pltpu.make_async_copy(ys_hbm.at[pl.ds(src_row, 1), :], buf.at[pl.ds(r, 1), :], sem)

        def issue(r, c):
            row_copy(r, pos_ref[0, r]).start()
            return c

        def drain(r, c):
            row_copy(r, 0).wait()
            return c

        lax.fori_loop(0, n_rows, issue, 0)
        lax.fori_loop(0, n_rows, drain, 0)
        rw = rw_ref[...]
        y = rw[:, 0:1] * buf[0:ROW_BLK, :] + rw[:, 1:2] * buf[ROW_BLK:n_rows, :]
    else:
        y = y_ref[...]

    x_in = _read_split(x_ref, xs_ref, split_in) if split_in else x_ref[...]
    x = x_in + gate_ref[...] * _rms(y, ga_ref[...])
    if split_out:
        @pl.when(pl.program_id(0) < split_out)
        def _():
            xo_ref[...] = x

        @pl.when(pl.program_id(0) >= split_out)
        def _():
            xos_ref[...] = x
    else:
        xo_ref[...] = x
    if has_next:
        h = _rms(x, gb_ref[...]) * (1.0 + sc_ref[...]) + sh_ref[...]
        h_ref[...] = h.astype(BF16)
        if route:
            hf_ref[...] = h
            logits = jnp.dot(h.astype(BF16), wr_ref[...].astype(BF16), preferred_element_type=F32)
            ridx, rw_out = _top2(logits)
            ridx_ref[...] = ridx
            rwo_ref[...] = rw_out


def _post(x, y, norm_g4, layer, ga_idx, mods, gate_idx, blocks_per_group, *,
          nxt=None, w_router_pad=None, gather=None, split_out=0):
    split_in = 0
    if isinstance(x, tuple):
        x_lead, x_tail = x
        split_in = x_lead.shape[0] // ROW_BLK
        m, d = x_lead.shape[0] + x_tail.shape[0], x_lead.shape[1]
    else:
        m, d = x.shape
    n_groups = mods.shape[1] // ROW_BLK
    row = pl.BlockSpec((ROW_BLK, d), lambda i: (i, 0))
    lane_row = pl.BlockSpec((ROW_BLK, LANES), lambda i: (i, 0))
    has_next = nxt is not None
    route = w_router_pad is not None
    args, specs, scratch = [], [], []
    if gather is not None:
        pos, rw = gather
        args += [pos, rw, y]
        specs += [pl.BlockSpec((None, 1, TOP_K * ROW_BLK), lambda i: (i, 0, 0), memory_space=pltpu.SMEM),
                  lane_row, pl.BlockSpec(memory_space=pl.ANY)]
        scratch = [pltpu.VMEM((TOP_K * ROW_BLK, d), F32), pltpu.SemaphoreType.DMA(())]
    else:
        args.append(y)
        specs.append(row)
    if split_in:
        args += [x_lead, x_tail]
        specs += _split_specs(split_in, d)
    else:
        args.append(x)
        specs.append(row)
    args += [norm_g4, mods]
    specs += [pl.BlockSpec((None, None, 1, d), lambda i: (layer, ga_idx, 0, 0)),
              _mod_spec(gate_idx, blocks_per_group, n_groups, d)]
    if split_out:
        out_shape = [jax.ShapeDtypeStruct((split_out * ROW_BLK, d), F32),
                     jax.ShapeDtypeStruct((m - split_out * ROW_BLK, d), F32)]
        out_specs = _split_specs(split_out, d)
    else:
        out_shape = [jax.ShapeDtypeStruct((m, d), F32)]
        out_specs = [row]
    if has_next:
        n_layer, n_gidx, n_mods, sc_idx, sh_idx = nxt
        args += [norm_g4, n_mods, n_mods]
        specs += [pl.BlockSpec((None, None, 1, d), lambda i: (n_layer, n_gidx, 0, 0)),
                  _mod_spec(sc_idx, blocks_per_group, n_groups, d),
                  _mod_spec(sh_idx, blocks_per_group, n_groups, d)]
        out_shape.append(jax.ShapeDtypeStruct((m, d), BF16))
        out_specs.append(row)
    if route:
        args.append(w_router_pad)
        specs.append(pl.BlockSpec((d, LANES), lambda i: (0, 0)))
        out_shape += [jax.ShapeDtypeStruct((m, d), F32), jax.ShapeDtypeStruct((m, LANES), jnp.int32),
                      jax.ShapeDtypeStruct((m, LANES), F32)]
        out_specs += [row, lane_row, lane_row]
    return pl.pallas_call(
        functools.partial(_post_kernel, has_next=has_next, route=route, gather=gather is not None,
                          split_in=split_in, split_out=split_out),
        grid=(m // ROW_BLK,),
        in_specs=specs,
        out_specs=out_specs,
        out_shape=out_shape,
        scratch_shapes=scratch,
        compiler_params=_cparams(("arbitrary",)),
        name="post_gather" if gather is not None else ("post_route" if route else "post"),
    )(*args)


def _row_resident(shape, index_map):
    return pl.BlockSpec(shape, index_map, pipeline_mode=pl.Buffered(1))


def _mm_kernel(x_ref, w_ref, o_ref):
    o_ref[...] = jnp.dot(x_ref[...], w_ref[...].astype(BF16), preferred_element_type=F32).astype(o_ref.dtype)


def _matmul(x, w_stack, layer, col_blk0, n_out, tm, tn):
    m, k = x.shape
    return pl.pallas_call(
        _mm_kernel,
        grid=(m // tm, n_out // tn),
        in_specs=[
            _row_resident((tm, k), lambda i, j: (i, 0)),
            pl.BlockSpec((None, k, tn), lambda i, j: (layer, 0, col_blk0 + j)),
        ],
        out_specs=pl.BlockSpec((tm, tn), lambda i, j: (i, j)),
        out_shape=jax.ShapeDtypeStruct((m, n_out), F32),
        compiler_params=_cparams(("arbitrary", "arbitrary")),
        name="matmul",
    )(x, w_stack)


def _mm2_kernel(a_ref, c_ref, w_ref, o_ref):
    ka = a_ref.shape[1]
    acc = jnp.dot(a_ref[...], w_ref[0:ka, :].astype(BF16), preferred_element_type=F32)
    acc = acc + jnp.dot(c_ref[...], w_ref[ka:, :].astype(BF16), preferred_element_type=F32)
    o_ref[...] = acc


def _matmul2(a, c, w_stack, layer, tm, tn):
    m, ka = a.shape
    kc = c.shape[1]
    n = w_stack.shape[2]
    return pl.pallas_call(
        _mm2_kernel,
        grid=(m // tm, n // tn),
        in_specs=[
            _row_resident((tm, ka), lambda i, j: (i, 0)),
            _row_resident((tm, kc), lambda i, j: (i, 0)),
            pl.BlockSpec((None, ka + kc, tn), lambda i, j: (layer, 0, j)),
        ],
        out_specs=pl.BlockSpec((tm, tn), lambda i, j: (i, j)),
        out_shape=jax.ShapeDtypeStruct((m, n), F32),
        compiler_params=_cparams(("arbitrary", "arbitrary")),
        name="matmul_out",
    )(a, c, w_stack)


SUB_ROWS = 208


def _swiglu(x, wg, wu):
    g = jnp.dot(x, wg, preferred_element_type=F32)
    u = jnp.dot(x, wu, preferred_element_type=F32)
    return (_silu(g) * u).astype(BF16)


def _gateup_body(x_ref, wg_ref, wu_ref, o_ref):
    o_ref[...] = _swiglu(x_ref[...], wg_ref[...].astype(BF16), wu_ref[...].astype(BF16))


def _for_sub_blocks(rows, total_rows, compute, clear):
    n_sub = total_rows // SUB_ROWS
    n_valid = (rows + SUB_ROWS - 1) // SUB_ROWS

    def do(fn):
        def body(r, c):
            fn(pl.multiple_of(r * SUB_ROWS, SUB_ROWS))
            return c
        return body

    lax.fori_loop(0, n_valid, do(compute), 0)
    lax.fori_loop(n_valid, n_sub, do(clear), 0)


def _gateup_kernel(x_ref, wg_ref, wu_ref, o_ref):
    _gateup_body(x_ref, wg_ref, wu_ref, o_ref)


def _gateup(x, wg_stack, wu_stack, layer, tm, tn):
    m, k = x.shape
    f = wg_stack.shape[2]
    w_spec = pl.BlockSpec((None, k, tn), lambda i, j: (layer, 0, j))
    return pl.pallas_call(
        _gateup_kernel,
        grid=(m // tm, f // tn),
        in_specs=[_row_resident((tm, k), lambda i, j: (i, 0)), w_spec, w_spec],
        out_specs=pl.BlockSpec((tm, tn), lambda i, j: (i, j)),
        out_shape=jax.ShapeDtypeStruct((m, f), BF16),
        compiler_params=_cparams(("arbitrary", "arbitrary")),
        name="gateup",
    )(x, wg_stack, wu_stack)


ACC_ROWS = 1040
ACC_COLS = 256


def _down_body(a_ref, w_ref, o_ref, k):
    @pl.when(k == 0)
    def _():
        o_ref[...] = jnp.zeros_like(o_ref)

    tm, tn = o_ref.shape
    for c0 in range(0, tn, ACC_COLS):
        w = w_ref[:, c0:c0 + ACC_COLS].astype(BF16)
        for r0 in range(0, tm, ACC_ROWS):
            o_ref[r0:r0 + ACC_ROWS, c0:c0 + ACC_COLS] += jnp.dot(
                a_ref[r0:r0 + ACC_ROWS, :], w, preferred_element_type=F32)


def _down_kernel(a_ref, w_ref, o_ref):
    _down_body(a_ref, w_ref, o_ref, pl.program_id(2))


def _down(a, wd_stack, layer, tm, tn, tk):
    m, f = a.shape
    d = wd_stack.shape[2]
    return pl.pallas_call(
        _down_kernel,
        grid=(m // tm, d // tn, f // tk),
        in_specs=[
            pl.BlockSpec((tm, tk), lambda i, j, k: (i, k)),
            pl.BlockSpec((None, tk, tn), lambda i, j, k: (layer, k, j)),
        ],
        out_specs=pl.BlockSpec((tm, tn), lambda i, j, k: (i, j)),
        out_shape=jax.ShapeDtypeStruct((m, d), F32),
        compiler_params=_cparams(("arbitrary", "arbitrary", "arbitrary")),
        name="down",
    )(a, wd_stack)


def _gateup_grouped_kernel(te_ref, tr_ref, nu_ref, x_ref, wg_ref, wu_ref, o_ref):
    tm = x_ref.shape[0]
    rows = tr_ref[pl.program_id(0)]

    @pl.when(rows == tm)
    def _():
        _gateup_body(x_ref, wg_ref, wu_ref, o_ref)

    @pl.when(rows < tm)
    def _():
        wg = wg_ref[...].astype(BF16)
        wu = wu_ref[...].astype(BF16)

        def compute(rs):
            o_ref[pl.ds(rs, SUB_ROWS), :] = _swiglu(x_ref[pl.ds(rs, SUB_ROWS), :], wg, wu)

        def clear(rs):
            o_ref[pl.ds(rs, SUB_ROWS), :] = jnp.zeros((SUB_ROWS, o_ref.shape[1]), o_ref.dtype)

        _for_sub_blocks(rows, tm, compute, clear)


def _gateup_grouped(xs, wg, wu, layer, tile_e, tile_rows, n_used, tm, tn):
    r, k = xs.shape
    f = wg.shape[3]
    nj = f // tn

    def w_map(i, j, te, tr, nu):
        return (layer, te[i], 0, jnp.where(i < nu[0], j, nj - 1))

    w_spec = pl.BlockSpec((None, None, k, tn), w_map)
    return pl.pallas_call(
        _gateup_grouped_kernel,
        grid_spec=pltpu.PrefetchScalarGridSpec(
            num_scalar_prefetch=3,
            grid=(r // tm, nj),
            in_specs=[_row_resident((tm, k), lambda i, j, te, tr, nu: (jnp.minimum(i, nu[0] - 1), 0)),
                      w_spec, w_spec],
            out_specs=pl.BlockSpec((tm, tn), lambda i, j, te, tr, nu: (i, j)),
        ),
        out_shape=jax.ShapeDtypeStruct((r, f), BF16),
        compiler_params=_cparams(("arbitrary", "arbitrary")),
        name="gateup_grouped",
    )(tile_e, tile_rows, n_used, xs, wg, wu)


def _down_grouped_kernel(te_ref, tr_ref, nu_ref, a_ref, w_ref, o_ref):
    tm = a_ref.shape[0]
    rows = tr_ref[pl.program_id(0)]
    k = pl.program_id(2)

    @pl.when(rows == tm)
    def _():
        _down_body(a_ref, w_ref, o_ref, k)

    @pl.when(rows < tm)
    def _():
        w = w_ref[...].astype(BF16)

        @pl.when(k == 0)
        def _():
            o_ref[...] = jnp.zeros_like(o_ref)

        def compute(rs):
            o_ref[pl.ds(rs, SUB_ROWS), :] += jnp.dot(a_ref[pl.ds(rs, SUB_ROWS), :], w, preferred_element_type=F32)

        _for_sub_blocks(rows, tm, compute, lambda rs: None)


def _down_grouped(a, wd, layer, tile_e, tile_rows, n_used, tm, tn, tk):
    r, f = a.shape
    d = wd.shape[3]
    nk = f // tk
    nj = d // tn

    def a_map(i, j, k, te, tr, nu):
        return (jnp.minimum(i, nu[0] - 1), jnp.where(i < nu[0], k, nk - 1))

    def w_map(i, j, k, te, tr, nu):
        used = i < nu[0]
        return (layer, te[i], jnp.where(used, k, nk - 1), jnp.where(used, j, nj - 1))

    return pl.pallas_call(
        _down_grouped_kernel,
        grid_spec=pltpu.PrefetchScalarGridSpec(
            num_scalar_prefetch=3,
            grid=(r // tm, nj, nk),
            in_specs=[pl.BlockSpec((tm, tk), a_map), pl.BlockSpec((None, None, tk, tn), w_map)],
            out_specs=pl.BlockSpec((tm, tn), lambda i, j, k, te, tr, nu: (i, j)),
        ),
        out_shape=jax.ShapeDtypeStruct((r, d), F32),
        compiler_params=_cparams(("arbitrary", "arbitrary", "arbitrary")),
        name="down_grouped",
    )(tile_e, tile_rows, n_used, a, wd)


def _dispatch_kernel(tr_ref, src_ref, h_hbm, o_ref, buf, sem, *, steps_per_tile):
    rows = o_ref.shape[0]
    step = pl.program_id(0)
    used = (step % steps_per_tile) * rows < tr_ref[step // steps_per_tile]

    def row_copy(r, src_row):
        return pltpu.make_async_copy(h_hbm.at[pl.ds(src_row, 1), :], buf.at[pl.ds(r, 1), :], sem)

    @pl.when(used)
    def _():
        def issue(r, c):
            row_copy(r, src_ref[0, r]).start()
            return c

        def drain(r, c):
            row_copy(r, 0).wait()
            return c

        lax.fori_loop(0, rows, issue, 0)
        lax.fori_loop(0, rows, drain, 0)
        o_ref[...] = buf[...].astype(BF16)

    @pl.when(jnp.logical_not(used))
    def _():
        o_ref[...] = jnp.zeros_like(o_ref)


def _dispatch(h_f32, src, tile_rows, tm, rows_per_step):
    d = h_f32.shape[1]
    r = src.shape[0]
    steps = r // rows_per_step
    return pl.pallas_call(
        functools.partial(_dispatch_kernel, steps_per_tile=tm // rows_per_step),
        grid_spec=pltpu.PrefetchScalarGridSpec(
            num_scalar_prefetch=1,
            grid=(steps,),
            in_specs=[
                pl.BlockSpec((None, 1, rows_per_step), lambda i, tr: (i, 0, 0), memory_space=pltpu.SMEM),
                pl.BlockSpec(memory_space=pl.ANY),
            ],
            out_specs=pl.BlockSpec((rows_per_step, d), lambda i, tr: (i, 0)),
            scratch_shapes=[pltpu.VMEM((rows_per_step, d), F32), pltpu.SemaphoreType.DMA(())],
        ),
        out_shape=jax.ShapeDtypeStruct((r, d), BF16),
        compiler_params=_cparams(("arbitrary",)),
        name="dispatch",
    )(tile_rows, src.reshape(steps, 1, rows_per_step), h_f32)


def _route_meta(ridx, tm, n_tiles):
    m = ridx.shape[0]
    e_flat = ridx[:, :TOP_K].reshape(-1)
    onehot = (e_flat[:, None] == jnp.arange(N_EXPERTS, dtype=jnp.int32)[None, :]).astype(jnp.int32)
    csum = jnp.cumsum(onehot, axis=0)
    rank = jnp.take_along_axis(csum, e_flat[:, None], axis=1)[:, 0] - 1
    counts = csum[-1]
    ptiles = (counts + tm - 1) // tm
    tile_end = jnp.cumsum(ptiles)
    tile_start = tile_end - ptiles
    pos = (tile_start[e_flat] * tm + rank).astype(jnp.int32)
    n_used = tile_end[-1].astype(jnp.int32)
    tiles = jnp.arange(n_tiles, dtype=jnp.int32)
    tile_e_raw = jnp.sum((tiles[:, None] >= tile_end[None, :]).astype(jnp.int32), axis=1)
    last_e = tile_e_raw[jnp.maximum(n_used - 1, 0)]
    tile_e = jnp.minimum(jnp.where(tiles < n_used, tile_e_raw, last_e), N_EXPERTS - 1).astype(jnp.int32)
    tile_rows = jnp.clip(counts[tile_e] - (tiles - tile_start[tile_e]) * tm, 0, tm)
    tile_rows = jnp.where(tiles < n_used, tile_rows, 0).astype(jnp.int32)
    src = jnp.zeros((n_tiles * tm,), jnp.int32).at[pos].set(jnp.arange(m * TOP_K, dtype=jnp.int32) // TOP_K)
    return pos.reshape(m, TOP_K), src, tile_e, tile_rows, n_used.reshape(1)


def _softmax_sink_pv(s, sink, vv):
    m = jnp.maximum(jnp.max(s, axis=1, keepdims=True), sink)
    p = jnp.exp(s - m)
    den = jnp.sum(p, axis=1, keepdims=True) + jnp.exp(sink - m)
    return jnp.dot((p / den).astype(BF16), vv, preferred_element_type=F32)


def _qk(q, kk):
    return lax.dot_general(q, kk, (((1,), (1,)), ((), ())), preferred_element_type=F32) * (1.0 / math.sqrt(HEAD_DIM))


def _attn_prompt_kernel(sink_ref, q_ref, kp_ref, kc_ref, vp_ref, vc_ref, bias_ref, tail_ref, o_ref, *,
                        layer, blocks_per_seq, n_blocks):
    i = pl.program_id(0)

    @pl.when(i < n_blocks)
    def _():
        first = (i % blocks_per_seq) == 0
        col = lax.broadcasted_iota(jnp.int32, (ROW_BLK, 2 * ROW_BLK), 1)
        no_prev = jnp.logical_and(first, col < ROW_BLK)
        for kvh in range(N_KV_HEADS):
            cs = slice(kvh * HEAD_DIM, (kvh + 1) * HEAD_DIM)
            kk = jnp.concatenate([kp_ref[:, cs], kc_ref[:, cs]], axis=0).astype(BF16)
            vv = jnp.concatenate([vp_ref[:, cs], vc_ref[:, cs]], axis=0).astype(BF16)
            for g in range(GQA_GROUP):
                h = kvh * GQA_GROUP + g
                hs = slice(h * HEAD_DIM, (h + 1) * HEAD_DIM)
                s = _qk(q_ref[:, hs].astype(BF16), kk) + bias_ref[h]
                s = jnp.where(no_prev, -jnp.inf, s)
                o_ref[:, hs] = _softmax_sink_pv(s, sink_ref[layer, h], vv).astype(BF16)

    @pl.when(i >= n_blocks)
    def _():
        o_ref[...] = tail_ref[...]


def _attn_prompt(qkv, sinks, bias_p, tail, layer, n_prompt_rows, blocks_per_seq):
    kb = D_ATTN // D_KV
    n_blocks = n_prompt_rows // ROW_BLK

    def prev(i):
        return jnp.maximum(i - 1, 0)

    return pl.pallas_call(
        functools.partial(_attn_prompt_kernel, layer=layer, blocks_per_seq=blocks_per_seq, n_blocks=n_blocks),
        grid=(n_blocks + 1,),
        in_specs=[
            pl.BlockSpec(memory_space=pltpu.SMEM),
            pl.BlockSpec((ROW_BLK, D_ATTN), lambda i: (i, 0)),
            pl.BlockSpec((ROW_BLK, D_KV), lambda i: (prev(i), kb)),
            pl.BlockSpec((ROW_BLK, D_KV), lambda i: (i, kb)),
            pl.BlockSpec((ROW_BLK, D_KV), lambda i: (prev(i), kb + 1)),
            pl.BlockSpec((ROW_BLK, D_KV), lambda i: (i, kb + 1)),
            pl.BlockSpec((N_HEADS, ROW_BLK, 2 * ROW_BLK), lambda i: (0, 0, 0)),
            pl.BlockSpec((ROW_BLK, D_ATTN), lambda i: (0, 0)),
        ],
        out_specs=pl.BlockSpec((ROW_BLK, D_ATTN), lambda i: (i, 0)),
        out_shape=jax.ShapeDtypeStruct((n_prompt_rows + ROW_BLK, D_ATTN), BF16),
        compiler_params=_cparams(("arbitrary",)),
        name="attn_prompt",
    )(sinks, qkv, qkv, qkv, qkv, qkv, bias_p, tail)


def _attn_sample_kernel(q_ref, ck_ref, cv_ref, kn_ref, vn_ref, bias_ref, sink_ref, o_ref):
    for kvh in range(N_KV_HEADS):
        kk = jnp.concatenate([ck_ref[kvh], kn_ref[kvh]], axis=0).astype(BF16)
        vv = jnp.concatenate([cv_ref[kvh], vn_ref[kvh]], axis=0).astype(BF16)
        s = _qk(q_ref[kvh].astype(BF16), kk) + bias_ref[kvh]
        o_ref[kvh] = _softmax_sink_pv(s, sink_ref[kvh], vv)


def _attn_sample(q, ck, cv, kn, vn, bias_s, sink_s):
    b, _, rows, _ = q.shape
    wb = ck.shape[2]
    pad = kn.shape[2]

    def per_b(n):
        return pl.BlockSpec((None, N_KV_HEADS, n, HEAD_DIM), lambda i: (i, 0, 0, 0))

    return pl.pallas_call(
        _attn_sample_kernel,
        grid=(b,),
        in_specs=[
            per_b(rows), per_b(wb), per_b(wb), per_b(pad), per_b(pad),
            pl.BlockSpec((N_KV_HEADS, rows, wb + pad), lambda i: (0, 0, 0)),
            pl.BlockSpec((N_KV_HEADS, rows, 1), lambda i: (0, 0, 0)),
        ],
        out_specs=per_b(rows),
        out_shape=jax.ShapeDtypeStruct(q.shape, F32),
        compiler_params=_cparams(("arbitrary",)),
        name="attn_sample",
    )(q, ck, cv, kn, vn, bias_s, sink_s)


def _conv_pool(ue_ref, ve_ref, conv_base, pool_base, stride, rows, gate_b, pv, cw_ref, pw_ref, ps_ref, cnt_of, o_ref):
    d_conv = gate_b.shape[1]
    y = cw_ref[CONV_W - 1:CONV_W, :] * ue_ref[conv_base:conv_base + rows, :]
    for j in range(1, CONV_W):
        off = conv_base - j * stride
        y = y + cw_ref[CONV_W - 1 - j:CONV_W - j, :] * ue_ref[off:off + rows, :]
    o_ref[:, 0:d_conv] = (gate_b * y).astype(BF16)
    grp = pv.shape[1] // len(POOL_WINDOWS)
    for gi, w in enumerate(POOL_WINDOWS):
        cs = slice(gi * grp, (gi + 1) * grp)
        acc = ve_ref[pool_base:pool_base + rows, cs]
        for j in range(1, w):
            off = pool_base - j * stride
            acc = acc + ve_ref[off:off + rows, cs]
        g = acc / cnt_of(w) - pv[:, cs]
        yg = jnp.dot(g.astype(BF16), pw_ref[gi].astype(BF16), preferred_element_type=F32) * ps_ref[:, cs]
        o_ref[:, d_conv + gi * grp:d_conv + (gi + 1) * grp] = yg.astype(BF16)


CONV_HALO = 8
POOL_HALO = 16


def _convpool_prompt_kernel(gb_ref, gc_ref, xt_ref, pv_ref, gch_ref, xth_ref, pvh_ref, cw_ref, pw_ref, ps_ref,
                            tail_ref, o_ref, ut_ref, ue_ref, ve_ref, *, blocks_per_seq, n_blocks):
    i = pl.program_id(0)

    @pl.when(i < n_blocks)
    def _():
        n = i % blocks_per_seq
        keep = (n != 0).astype(F32)
        u = gc_ref[...] * xt_ref[...]
        ue_ref[0:CONV_HALO, :] = gch_ref[...] * xth_ref[...] * keep
        ue_ref[CONV_HALO:CONV_HALO + ROW_BLK, :] = u
        ut_ref[...] = u[ROW_BLK - CONV_HALO:ROW_BLK, :]
        pv = pv_ref[...]
        ve_ref[0:POOL_HALO, :] = pvh_ref[...] * keep
        ve_ref[POOL_HALO:POOL_HALO + ROW_BLK, :] = pv
        pos1 = (n * ROW_BLK + 1 + lax.broadcasted_iota(jnp.int32, (ROW_BLK, 1), 0)).astype(F32)

        def cnt_of(w):
            return jnp.minimum(pos1, float(w))

        _conv_pool(ue_ref, ve_ref, CONV_HALO, POOL_HALO, 1, ROW_BLK, gb_ref[...], pv, cw_ref, pw_ref, ps_ref,
                   cnt_of, o_ref)

    @pl.when(i >= n_blocks)
    def _():
        o_ref[...] = tail_ref[...]


def _convpool_prompt(rest, tail, conv_w, pool_w, pool_scale3, layer, n_prompt_rows, blocks_per_seq):
    c = rest.shape[1] // 4
    n_blocks = n_prompt_rows // ROW_BLK
    n_seq = n_blocks // blocks_per_seq
    ch, ph = ROW_BLK // CONV_HALO, ROW_BLK // POOL_HALO

    def blk(col):
        return pl.BlockSpec((ROW_BLK, c), lambda i: (i, col))

    def halo(rows, per_blk, col):
        return pl.BlockSpec((rows, c), lambda i: (jnp.maximum(i * per_blk - 1, 0), col))

    return pl.pallas_call(
        functools.partial(_convpool_prompt_kernel, blocks_per_seq=blocks_per_seq, n_blocks=n_blocks),
        grid=(n_blocks + 1,),
        in_specs=[
            blk(0), blk(1), blk(2), blk(3),
            halo(CONV_HALO, ch, 1), halo(CONV_HALO, ch, 2), halo(POOL_HALO, ph, 3),
            pl.BlockSpec((None, CONV_W, c), lambda i: (layer, 0, 0)),
            pl.BlockSpec((None, len(POOL_WINDOWS), c // 4, c // 4), lambda i: (layer, 0, 0, 0)),
            pl.BlockSpec((None, 1, c), lambda i: (layer, 0, 0)),
            pl.BlockSpec((ROW_BLK, 2 * c), lambda i: (0, 0)),
        ],
        out_specs=[
            pl.BlockSpec((ROW_BLK, 2 * c), lambda i: (i, 0)),
            pl.BlockSpec((CONV_HALO, c), lambda i: (jnp.minimum(i // blocks_per_seq, n_seq - 1), 0)),
        ],
        out_shape=[jax.ShapeDtypeStruct((n_prompt_rows + ROW_BLK, 2 * c), BF16),
                   jax.ShapeDtypeStruct((n_seq * CONV_HALO, c), F32)],
        scratch_shapes=[pltpu.VMEM((CONV_HALO + ROW_BLK, c), F32), pltpu.VMEM((POOL_HALO + ROW_BLK, c), F32)],
        compiler_params=_cparams(("arbitrary",)),
        name="convpool_prompt",
    )(rest, rest, rest, rest, rest, rest, rest, conv_w, pool_w, pool_scale3, tail)


def _convpool_sample_kernel(gb_ref, gc_ref, xt_ref, pv_ref, cprev_ref, pprev_ref, cw_ref, pw_ref, ps_ref,
                            o_ref, u_ref, ue_ref, ve_ref, *, stride, pos0):
    rows = gb_ref.shape[0]
    ch, ph = cprev_ref.shape[0], pprev_ref.shape[0]
    u = gc_ref[...] * xt_ref[...]
    u_ref[...] = u
    ue_ref[0:ch, :] = cprev_ref[...]
    ue_ref[ch:ch + rows, :] = u
    pv = pv_ref[...]
    ve_ref[0:ph, :] = pprev_ref[...]
    ve_ref[ph:ph + rows, :] = pv
    pos1 = (pos0 + 1 + lax.broadcasted_iota(jnp.int32, (rows, 1), 0) // stride).astype(F32)

    def cnt_of(w):
        return jnp.minimum(pos1, float(w))

    _conv_pool(ue_ref, ve_ref, ch, ph, stride, rows, gb_ref[...], pv, cw_ref, pw_ref, ps_ref, cnt_of, o_ref)


def _convpool_sample(rest, row_blk_idx, cprev, pprev, conv_w, pool_w, pool_scale3, layer, stride, pos0):
    c = rest.shape[1] // 4
    rows = ROW_BLK

    def blk(col):
        return pl.BlockSpec((rows, c), lambda i: (row_blk_idx, col))

    return pl.pallas_call(
        functools.partial(_convpool_sample_kernel, stride=stride, pos0=pos0),
        grid=(1,),
        in_specs=[
            blk(0), blk(1), blk(2), blk(3),
            pl.BlockSpec(cprev.shape, lambda i: (0, 0)),
            pl.BlockSpec(pprev.shape, lambda i: (0, 0)),
            pl.BlockSpec((None, CONV_W, c), lambda i: (layer, 0, 0)),
            pl.BlockSpec((None, len(POOL_WINDOWS), c // 4, c // 4), lambda i: (layer, 0, 0, 0)),
            pl.BlockSpec((None, 1, c), lambda i: (layer, 0, 0)),
        ],
        out_specs=[pl.BlockSpec((rows, 2 * c), lambda i: (0, 0)), pl.BlockSpec((rows, c), lambda i: (0, 0))],
        out_shape=[jax.ShapeDtypeStruct((rows, 2 * c), BF16), jax.ShapeDtypeStruct((rows, c), F32)],
        scratch_shapes=[pltpu.VMEM((cprev.shape[0] + rows, c), F32), pltpu.VMEM((pprev.shape[0] + rows, c), F32)],
        compiler_params=_cparams(("arbitrary",)),
        name="convpool_sample",
    )(rest, rest, rest, rest, cprev, pprev, conv_w, pool_w, pool_scale3)


TM = 2080
TM_MOE = 2080
TN_PROJ = 512
TN_FF = 256
TN_DOWN = 1024
TK_DOWN = 1024
DISPATCH_ROWS = SUB_ROWS
KN_PAD = 16


def kernel(x_prompt, x_sample, cache_k, cache_v, state_conv, state_pool, c_prompt, c_sample, rel_bias, w_ada, b_ada, norm_g, w_in, sinks, conv_w, pool_w, pool_scale, w_out, w_gate_dense, w_up_dense, w_down_dense, w_router, w_gate_moe, w_up_moe, w_down_moe):
    bp, t, d = x_prompt.shape
    bs, s, _ = x_sample.shape
    depth = w_ada.shape[0]
    wb = cache_k.shape[2]
    n_p = bp * t
    n_s = bs * s
    m = n_p + n_s
    assert n_s == ROW_BLK and t % ROW_BLK == 0 and m % TM == 0
    blocks_per_seq = t // ROW_BLK
    d_qkv = D_ATTN + 2 * D_KV
    c_ch = (w_in.shape[2] - d_qkv) // 4

    x = (x_prompt.reshape(n_p, d), x_sample.transpose(1, 0, 2).reshape(n_s, d))

    n_c = bp + bs
    c_rows = -(-n_c // 16) * 16
    c_all = jnp.concatenate([c_prompt, c_sample, jnp.zeros((c_rows - n_c, d), F32)], axis=0)
    mod = _ada_mod(c_all, w_ada, b_ada)

    def mod_rows(l):
        ml = mod[l].reshape(c_rows, N_MOD, d).transpose(1, 0, 2)
        mp = jnp.broadcast_to(ml[:, :bp, None, :], (N_MOD, bp, ROW_BLK, d)).reshape(N_MOD, bp * ROW_BLK, d)
        ms = jnp.tile(ml[:, bp:bp + bs], (1, s, 1))
        return jnp.concatenate([mp, ms], axis=1)

    mods = [mod_rows(l) for l in range(depth)]
    norm_g4 = norm_g.reshape(depth, 4, 1, d)

    qi = jnp.arange(ROW_BLK)[:, None]
    dist_p = ROW_BLK + qi - jnp.arange(2 * ROW_BLK)[None, :]
    bias_p = _bias_table(rel_bias, _masked_bucket(dist_p))
    kn_pad = KN_PAD
    si = jnp.arange(8)[:, None]
    dist_s = si + wb - jnp.arange(wb + kn_pad)[None, :]
    bucket_s = jnp.where((si < s) & (jnp.arange(wb + kn_pad)[None, :] < wb + s), _masked_bucket(dist_s), -1)
    bias_s = _bias_table(rel_bias, bucket_s)[:, :s]
    bias_s = bias_s.reshape(N_KV_HEADS, GQA_GROUP, s, wb + kn_pad).transpose(0, 2, 1, 3)
    bias_s = bias_s.reshape(N_KV_HEADS, s * GQA_GROUP, wb + kn_pad)

    pool_scale3 = pool_scale.reshape(depth, 1, c_ch)
    w_router_pad = jnp.pad(w_router, ((0, 0), (0, 0), (0, LANES - N_EXPERTS)))

    h = _norm_mod(x[0], x[1], norm_g4, 0, mods[0], blocks_per_seq)
    kp_l, vp_l, cp_l, pp_l, ks_l, vs_l, cs_l, ps_l = ([] for _ in range(8))
    for l in range(depth):
        j = l // 2
        is_moe = l % 2 == 1
        qkv = _matmul(h, w_in, l, 0, d_qkv, TM, TN_PROJ)
        rest = _matmul(h, w_in, l, d_qkv // TN_PROJ, 4 * c_ch, TM, TN_PROJ)

        qs = qkv[n_p:, :D_ATTN].reshape(s, bs, N_KV_HEADS, GQA_GROUP, HEAD_DIM)
        qs = qs.transpose(1, 2, 0, 3, 4).reshape(bs, N_KV_HEADS, s * GQA_GROUP, HEAD_DIM)
        k_new = qkv[n_p:, D_ATTN:D_ATTN + D_KV].reshape(s, bs, N_KV_HEADS, HEAD_DIM).transpose(1, 0, 2, 3)
        v_new = qkv[n_p:, D_ATTN + D_KV:].reshape(s, bs, N_KV_HEADS, HEAD_DIM).transpose(1, 0, 2, 3)
        pad_new = lambda a: jnp.pad(a.transpose(0, 2, 1, 3), ((0, 0), (0, 0), (0, kn_pad - s), (0, 0)))
        sink_s = jnp.tile(sinks[l].reshape(N_KV_HEADS, 1, GQA_GROUP), (1, s, 1)).reshape(N_KV_HEADS, s * GQA_GROUP, 1)
        o_s = _attn_sample(qs, cache_k[l].transpose(0, 2, 1, 3), cache_v[l].transpose(0, 2, 1, 3),
                           pad_new(k_new), pad_new(v_new), bias_s, sink_s)
        o_s = o_s.reshape(bs, N_KV_HEADS, s, GQA_GROUP, HEAD_DIM).transpose(2, 0, 1, 3, 4).reshape(n_s, D_ATTN)
        a_all = _attn_prompt(qkv, sinks, bias_p, o_s.astype(BF16), l, n_p, blocks_per_seq)

        cprev = state_conv[l].transpose(1, 0, 2).reshape((CONV_W - 1) * bs, c_ch)
        pprev = state_pool[l].transpose(1, 0, 2).reshape(POOL_PAD * bs, c_ch)
        cp_s, u_s = _convpool_sample(rest, n_p // ROW_BLK, cprev, pprev, conv_w, pool_w, pool_scale3, l, bs, PAST_LEN)
        cp_all, u_tail = _convpool_prompt(rest, cp_s, conv_w, pool_w, pool_scale3, l, n_p, blocks_per_seq)

        def seq_tails(a, n_rows, c0, c1):
            return jnp.stack([a[(b + 1) * t - n_rows:(b + 1) * t, c0:c1] for b in range(bp)])

        kp_l.append(seq_tails(qkv, WINDOW, D_ATTN, D_ATTN + D_KV).reshape(bp, WINDOW, N_KV_HEADS, HEAD_DIM))
        vp_l.append(seq_tails(qkv, WINDOW, D_ATTN + D_KV, d_qkv).reshape(bp, WINDOW, N_KV_HEADS, HEAD_DIM))
        cp_l.append(u_tail.reshape(bp, CONV_HALO, c_ch)[:, CONV_HALO - (CONV_W - 1):])
        pp_l.append(seq_tails(rest, POOL_PAD, 3 * c_ch, 4 * c_ch))
        ks_l.append(jnp.concatenate([cache_k[l], k_new], axis=1)[:, s:])
        vs_l.append(jnp.concatenate([cache_v[l], v_new], axis=1)[:, s:])
        u_bs = u_s.reshape(s, bs, c_ch).transpose(1, 0, 2)
        cs_l.append(jnp.concatenate([state_conv[l], u_bs], axis=1)[:, s:])
        pv_bs = rest[n_p:, 3 * c_ch:].reshape(s, bs, c_ch).transpose(1, 0, 2)
        ps_l.append(jnp.concatenate([state_pool[l], pv_bs], axis=1)[:, s:])

        mo = _matmul2(a_all, cp_all, w_out, l, TM, TN_PROJ)
        nxt2 = (l, 2, mods[l], 4, 3)
        last = l == depth - 1
        nxt_layer = None if last else (l + 1, 0, mods[l + 1], 1, 0)
        split_out = n_p // ROW_BLK if last else 0
        if not is_moe:
            x, h2 = _post(x, mo, norm_g4, l, 1, mods[l], 2, blocks_per_seq, nxt=nxt2)
            act = _gateup(h2, w_gate_dense, w_up_dense, j, TM, TN_FF)
            y = _down(act, w_down_dense, j, TM, TN_DOWN, TK_DOWN)
            outs = _post(x, y, norm_g4, l, 3, mods[l], 5, blocks_per_seq, nxt=nxt_layer, split_out=split_out)
        else:
            x, h2, h2f, ridx, rw = _post(x, mo, norm_g4, l, 1, mods[l], 2, blocks_per_seq, nxt=nxt2,
                                          w_router_pad=w_router_pad[j])
            n_tiles = (m * TOP_K) // TM_MOE + N_EXPERTS
            pos, src, tile_e, tile_rows, n_used = _route_meta(ridx, TM_MOE, n_tiles)
            hs = _dispatch(h2f, src, tile_rows, TM_MOE, DISPATCH_ROWS)
            act = _gateup_grouped(hs, w_gate_moe, w_up_moe, j, tile_e, tile_rows, n_used, TM_MOE, TN_FF)
            ys = _down_grouped(act, w_down_moe, j, tile_e, tile_rows, n_used, TM_MOE, TN_DOWN, TK_DOWN)
            pos_blk = pos.reshape(m // ROW_BLK, ROW_BLK, TOP_K).transpose(0, 2, 1).reshape(m // ROW_BLK, 1, TOP_K * ROW_BLK)
            outs = _post(x, ys, norm_g4, l, 3, mods[l], 5, blocks_per_seq, nxt=nxt_layer, gather=(pos_blk, rw),
                         split_out=split_out)
        if last:
            x = tuple(outs)
        else:
            x, h = outs

    y_prompt = x[0].reshape(bp, t, d)
    y_sample = x[1].reshape(s, bs, d).transpose(1, 0, 2)
    st = jnp.stack
    return (y_prompt, y_sample, st(kp_l), st(vp_l), st(cp_l), st(pp_l), st(ks_l), st(vs_l), st(cs_l), st(ps_l))
```

```python
import functools
import math

import jax
import jax.numpy as jnp
from jax import lax
from jax.experimental import pallas as pl
from jax.experimental.pallas import tpu as pltpu

F32 = jnp.float32
BF16 = jnp.bfloat16

N_HEADS = 32
N_KV_HEADS = 4
HEAD_DIM = 64
GQA_GROUP = N_HEADS // N_KV_HEADS
D_ATTN = N_HEADS * HEAD_DIM
D_KV = N_KV_HEADS * HEAD_DIM
WINDOW = 128
N_BUCKETS = 32
MAX_EXACT = N_BUCKETS // 2
MAX_DISTANCE = WINDOW
CONV_W = 3
POOL_WINDOWS = (2, 4, 8, 16)
POOL_PAD = max(POOL_WINDOWS) - 1
N_EXPERTS = 8
TOP_K = 2
N_MOD = 6
RMS_EPS = 1e-6
PAST_LEN = 16384

ROW_BLK = 128
LANES = 128
VMEM_LIMIT = 56 * 1024 * 1024


def _cparams(sem):
    return pltpu.CompilerParams(dimension_semantics=sem, vmem_limit_bytes=VMEM_LIMIT)


def _rms(x, g):
    return x * lax.rsqrt(jnp.mean(x * x, axis=-1, keepdims=True) + RMS_EPS) * g


def _silu(x):
    return x / (1.0 + jnp.exp(-x))


def _ada_kernel(c_ref, w_ref, b_ref, o_ref):
    s = _silu(c_ref[...]).astype(BF16)
    o_ref[...] = jnp.dot(s, w_ref[...].astype(BF16), preferred_element_type=F32) + b_ref[...]


def _ada_mod(c_pad, w_ada, b_ada, tn=512):
    depth, d, n = w_ada.shape
    rows = c_pad.shape[0]
    return pl.pallas_call(
        _ada_kernel,
        grid=(depth, n // tn),
        in_specs=[
            pl.BlockSpec((rows, d), lambda l, j: (0, 0)),
            pl.BlockSpec((None, d, tn), lambda l, j: (l, 0, j)),
            pl.BlockSpec((None, 1, tn), lambda l, j: (l, 0, j)),
        ],
        out_specs=pl.BlockSpec((None, rows, tn), lambda l, j: (l, 0, j)),
        out_shape=jax.ShapeDtypeStruct((depth, rows, n), F32),
        compiler_params=_cparams(("arbitrary", "arbitrary")),
        name="ada_mod",
    )(c_pad, w_ada, b_ada.reshape(depth, 1, n))


def _bias_kernel(rb_ref, bucket_ref, o_ref):
    h = pl.program_id(0)
    bucket = bucket_ref[...]
    acc = jnp.full(bucket.shape, -jnp.inf, F32)
    for b in range(N_BUCKETS):
        acc = jnp.where(bucket == b, rb_ref[b, h], acc)
    o_ref[...] = acc


def _bias_table(rel_bias, bucket):
    rq, rk = bucket.shape
    return pl.pallas_call(
        _bias_kernel,
        grid=(N_HEADS,),
        in_specs=[
            pl.BlockSpec(memory_space=pltpu.SMEM),
            pl.BlockSpec((rq, rk), lambda h: (0, 0)),
        ],
        out_specs=pl.BlockSpec((None, rq, rk), lambda h: (h, 0, 0)),
        out_shape=jax.ShapeDtypeStruct((N_HEADS, rq, rk), F32),
        compiler_params=_cparams(("arbitrary",)),
        name="bias_table",
    )(rel_bias, bucket)


def _t5_bucket(dist):
    n = jnp.maximum(dist, 0)
    nf = jnp.maximum(n, 1).astype(F32)
    large = MAX_EXACT + (jnp.log(nf / MAX_EXACT) / math.log(MAX_DISTANCE / MAX_EXACT)
                         * (N_BUCKETS - MAX_EXACT)).astype(jnp.int32)
    large = jnp.minimum(large, N_BUCKETS - 1)
    return jnp.where(n < MAX_EXACT, n, large)


def _masked_bucket(dist):
    return jnp.where((dist >= 0) & (dist < WINDOW), _t5_bucket(dist), -1).astype(jnp.int32)


def _mod_spec(which, blocks_per_group, n_groups, d):
    last = n_groups - 1
    return pl.BlockSpec((None, ROW_BLK, d),
                        lambda i: (which, jnp.minimum(i // blocks_per_group, last), 0))


def _split_specs(n_lead_blocks, d):
    lead = pl.BlockSpec((ROW_BLK, d), lambda i: (jnp.minimum(i, n_lead_blocks - 1), 0))
    tail = pl.BlockSpec((ROW_BLK, d), lambda i: (0, 0))
    return [lead, tail]


def _read_split(lead_ref, tail_ref, n_lead_blocks):
    return jnp.where(pl.program_id(0) < n_lead_blocks, lead_ref[...], tail_ref[...])


def _norm_mod_kernel(xp_ref, xs_ref, g_ref, sc_ref, sh_ref, h_ref, *, n_lead_blocks):
    x = _read_split(xp_ref, xs_ref, n_lead_blocks)
    h = _rms(x, g_ref[...]) * (1.0 + sc_ref[...]) + sh_ref[...]
    h_ref[...] = h.astype(BF16)


def _norm_mod(x_lead, x_tail, norm_g4, layer, mods, blocks_per_group):
    n_lead, d = x_lead.shape
    m = n_lead + x_tail.shape[0]
    n_groups = mods.shape[1] // ROW_BLK
    return pl.pallas_call(
        functools.partial(_norm_mod_kernel, n_lead_blocks=n_lead // ROW_BLK),
        grid=(m // ROW_BLK,),
        in_specs=_split_specs(n_lead // ROW_BLK, d) + [
            pl.BlockSpec((None, None, 1, d), lambda i: (layer, 0, 0, 0)),
            _mod_spec(1, blocks_per_group, n_groups, d),
            _mod_spec(0, blocks_per_group, n_groups, d),
        ],
        out_specs=pl.BlockSpec((ROW_BLK, d), lambda i: (i, 0)),
        out_shape=jax.ShapeDtypeStruct((m, d), BF16),
        compiler_params=_cparams(("arbitrary",)),
        name="norm_mod",
    )(x_lead, x_tail, norm_g4, mods, mods)


def _top2(logits):
    lane = lax.broadcasted_iota(jnp.int32, logits.shape, 1).astype(F32)
    lg = jnp.where(lane < N_EXPERTS, logits, -jnp.inf)
    m1 = jnp.max(lg, axis=1, keepdims=True)
    i1 = jnp.min(jnp.where(lg == m1, lane, float(LANES)), axis=1, keepdims=True)
    lg2 = jnp.where(lane == i1, -jnp.inf, lg)
    m2 = jnp.max(lg2, axis=1, keepdims=True)
    i2 = jnp.min(jnp.where(lg2 == m2, lane, float(LANES)), axis=1, keepdims=True)
    e2 = jnp.exp(m2 - m1)
    w1 = 1.0 / (1.0 + e2)
    w2 = e2 / (1.0 + e2)
    ridx = jnp.where(lane == 0, i1, jnp.where(lane == 1, i2, 0.0)).astype(jnp.int32)
    rw = jnp.where(lane == 0, w1, jnp.where(lane == 1, w2, 0.0))
    return ridx, rw


def _post_kernel(*refs, has_next, route, gather, split_in, split_out):
    it = iter(refs)
    if gather:
        pos_ref, rw_ref, ys_hbm = next(it), next(it), next(it)
    else:
        y_ref = next(it)
    x_ref = next(it)
    if split_in:
        xs_ref = next(it)
    ga_ref, gate_ref = next(it), next(it)
    if has_next:
        gb_ref, sc_ref, sh_ref = next(it), next(it), next(it)
    if route:
        wr_ref = next(it)
    xo_ref = next(it)
    if split_out:
        xos_ref = next(it)
    if has_next:
        h_ref = next(it)
    if route:
        hf_ref, ridx_ref, rwo_ref = next(it), next(it), next(it)
    if gather:
        buf, sem = next(it), next(it)

    if gather:
        n_rows = TOP_K * ROW_BLK

        def row_copy(r, src_row):
            return pltpu.make_async_copy(ys_hbm.at[pl.ds(src_row, 1), :], buf.at[pl.ds(r, 1), :], sem)

        def issue(r, c):
            row_copy(r, pos_ref[0, r]).start()
            return c

        def drain(r, c):
            row_copy(r, 0).wait()
            return c

        lax.fori_loop(0, n_rows, issue, 0)
        lax.fori_loop(0, n_rows, drain, 0)
        rw = rw_ref[...]
        y = rw[:, 0:1] * buf[0:ROW_BLK, :] + rw[:, 1:2] * buf[ROW_BLK:n_rows, :]
    else:
        y = y_ref[...]

    x_in = _read_split(x_ref, xs_ref, split_in) if split_in else x_ref[...]
    x = x_in + gate_ref[...] * _rms(y, ga_ref[...])
    if split_out:
        @pl.when(pl.program_id(0) < split_out)
        def _():
            xo_ref[...] = x

        @pl.when(pl.program_id(0) >= split_out)
        def _():
            xos_ref[...] = x
    else:
        xo_ref[...] = x
    if has_next:
        h = _rms(x, gb_ref[...]) * (1.0 + sc_ref[...]) + sh_ref[...]
        h_ref[...] = h.astype(BF16)
        if route:
            hf_ref[...] = h
            logits = jnp.dot(h.astype(BF16), wr_ref[...].astype(BF16), preferred_element_type=F32)
            ridx, rw_out = _top2(logits)
            ridx_ref[...] = ridx
            rwo_ref[...] = rw_out


def _post(x, y, norm_g4, layer, ga_idx, mods, gate_idx, blocks_per_group, *,
          nxt=None, w_router_pad=None, gather=None, split_out=0):
    split_in = 0
    if isinstance(x, tuple):
        x_lead, x_tail = x
        split_in = x_lead.shape[0] // ROW_BLK
        m, d = x_lead.shape[0] + x_tail.shape[0], x_lead.shape[1]
    else:
        m, d = x.shape
    n_groups = mods.shape[1] // ROW_BLK
    row = pl.BlockSpec((ROW_BLK, d), lambda i: (i, 0))
    lane_row = pl.BlockSpec((ROW_BLK, LANES), lambda i: (i, 0))
    has_next = nxt is not None
    route = w_router_pad is not None
    args, specs, scratch = [], [], []
    if gather is not None:
        pos, rw = gather
        args += [pos, rw, y]
        specs += [pl.BlockSpec((None, 1, TOP_K * ROW_BLK), lambda i: (i, 0, 0), memory_space=pltpu.SMEM),
                  lane_row, pl.BlockSpec(memory_space=pl.ANY)]
        scratch = [pltpu.VMEM((TOP_K * ROW_BLK, d), F32), pltpu.SemaphoreType.DMA(())]
    else:
        args.append(y)
        specs.append(row)
    if split_in:
        args += [x_lead, x_tail]
        specs += _split_specs(split_in, d)
    else:
        args.append(x)
        specs.append(row)
    args += [norm_g4, mods]
    specs += [pl.BlockSpec((None, None, 1, d), lambda i: (layer, ga_idx, 0, 0)),
              _mod_spec(gate_idx, blocks_per_group, n_groups, d)]
    if split_out:
        out_shape = [jax.ShapeDtypeStruct((split_out * ROW_BLK, d), F32),
                     jax.ShapeDtypeStruct((m - split_out * ROW_BLK, d), F32)]
        out_specs = _split_specs(split_out, d)
    else:
        out_shape = [jax.ShapeDtypeStruct((m, d), F32)]
        out_specs = [row]
    if has_next:
        n_layer, n_gidx, n_mods, sc_idx, sh_idx = nxt
        args += [norm_g4, n_mods, n_mods]
        specs += [pl.BlockSpec((None, None, 1, d), lambda i: (n_layer, n_gidx, 0, 0)),
                  _mod_spec(sc_idx, blocks_per_group, n_groups, d),
                  _mod_spec(sh_idx, blocks_per_group, n_groups, d)]
        out_shape.append(jax.ShapeDtypeStruct((m, d), BF16))
        out_specs.append(row)
    if route:
        args.append(w_router_pad)
        specs.append(pl.BlockSpec((d, LANES), lambda i: (0, 0)))
        out_shape += [jax.ShapeDtypeStruct((m, d), F32), jax.ShapeDtypeStruct((m, LANES), jnp.int32),
                      jax.ShapeDtypeStruct((m, LANES), F32)]
        out_specs += [row, lane_row, lane_row]
    return pl.pallas_call(
        functools.partial(_post_kernel, has_next=has_next, route=route, gather=gather is not None,
                          split_in=split_in, split_out=split_out),
        grid=(m // ROW_BLK,),
        in_specs=specs,
        out_specs=out_specs,
        out_shape=out_shape,
        scratch_shapes=scratch,
        compiler_params=_cparams(("arbitrary",)),
        name="post_gather" if gather is not None else ("post_route" if route else "post"),
    )(*args)


def _row_resident(shape, index_map):
    return pl.BlockSpec(shape, index_map, pipeline_mode=pl.Buffered(1))


def _mm_kernel(x_ref, w_ref, o_ref):
    o_ref[...] = jnp.dot(x_ref[...], w_ref[...].astype(BF16), preferred_element_type=F32).astype(o_ref.dtype)


def _matmul(x, w_stack, layer, col_blk0, n_out, tm, tn):
    m, k = x.shape
    return pl.pallas_call(
        _mm_kernel,
        grid=(m // tm, n_out // tn),
        in_specs=[
            _row_resident((tm, k), lambda i, j: (i, 0)),
            pl.BlockSpec((None, k, tn), lambda i, j: (layer, 0, col_blk0 + j)),
        ],
        out_specs=pl.BlockSpec((tm, tn), lambda i, j: (i, j)),
        out_shape=jax.ShapeDtypeStruct((m, n_out), F32),
        compiler_params=_cparams(("arbitrary", "arbitrary")),
        name="matmul",
    )(x, w_stack)


def _mm2_kernel(a_ref, c_ref, w_ref, o_ref):
    ka = a_ref.shape[1]
    acc = jnp.dot(a_ref[...], w_ref[0:ka, :].astype(BF16), preferred_element_type=F32)
    acc = acc + jnp.dot(c_ref[...], w_ref[ka:, :].astype(BF16), preferred_element_type=F32)
    o_ref[...] = acc


def _matmul2(a, c, w_stack, layer, tm, tn):
    m, ka = a.shape
    kc = c.shape[1]
    n = w_stack.shape[2]
    return pl.pallas_call(
        _mm2_kernel,
        grid=(m // tm, n // tn),
        in_specs=[
            _row_resident((tm, ka), lambda i, j: (i, 0)),
            _row_resident((tm, kc), lambda i, j: (i, 0)),
            pl.BlockSpec((None, ka + kc, tn), lambda i, j: (layer, 0, j)),
        ],
        out_specs=pl.BlockSpec((tm, tn), lambda i, j: (i, j)),
        out_shape=jax.ShapeDtypeStruct((m, n), F32),
        compiler_params=_cparams(("arbitrary", "arbitrary")),
        name="matmul_out",
    )(a, c, w_stack)


SUB_ROWS = 208


def _swiglu(x, wg, wu):
    g = jnp.dot(x, wg, preferred_element_type=F32)
    u = jnp.dot(x, wu, preferred_element_type=F32)
    return (_silu(g) * u).astype(BF16)


def _gateup_body(x_ref, wg_ref, wu_ref, o_ref):
    o_ref[...] = _swiglu(x_ref[...], wg_ref[...].astype(BF16), wu_ref[...].astype(BF16))


def _tile_plan(rows, tm):
    n_big_sub = tm // SUB_ROWS - 1
    n_valid = (rows + SUB_ROWS - 1) // SUB_ROWS
    use_big = n_valid >= (n_big_sub * 4) // 5
    return use_big, jnp.where(use_big, n_big_sub, 0), n_valid


def _for_sub_blocks(lo, hi, fn):
    def body(r, c):
        fn(pl.multiple_of(r * SUB_ROWS, SUB_ROWS))
        return c

    lax.fori_loop(lo, hi, body, 0)


def _gateup_kernel(x_ref, wg_ref, wu_ref, o_ref):
    _gateup_body(x_ref, wg_ref, wu_ref, o_ref)


def _gateup(x, wg_stack, wu_stack, layer, tm, tn):
    m, k = x.shape
    f = wg_stack.shape[2]
    w_spec = pl.BlockSpec((None, k, tn), lambda i, j: (layer, 0, j))
    return pl.pallas_call(
        _gateup_kernel,
        grid=(m // tm, f // tn),
        in_specs=[_row_resident((tm, k), lambda i, j: (i, 0)), w_spec, w_spec],
        out_specs=pl.BlockSpec((tm, tn), lambda i, j: (i, j)),
        out_shape=jax.ShapeDtypeStruct((m, f), BF16),
        compiler_params=_cparams(("arbitrary", "arbitrary")),
        name="gateup",
    )(x, wg_stack, wu_stack)


ACC_ROWS = 1040
ACC_COLS = 256


def _down_accumulate(a_ref, w, o_ref, n_rows):
    tn = o_ref.shape[1]
    for c0 in range(0, tn, ACC_COLS):
        wc = w[:, c0:c0 + ACC_COLS]
        for r0 in range(0, n_rows, ACC_ROWS):
            r1 = min(r0 + ACC_ROWS, n_rows)
            o_ref[r0:r1, c0:c0 + ACC_COLS] += jnp.dot(a_ref[r0:r1, :], wc, preferred_element_type=F32)


def _down_kernel(a_ref, w_ref, o_ref):
    @pl.when(pl.program_id(2) == 0)
    def _():
        o_ref[...] = jnp.zeros_like(o_ref)

    _down_accumulate(a_ref, w_ref[...].astype(BF16), o_ref, o_ref.shape[0])


def _down(a, wd_stack, layer, tm, tn, tk):
    m, f = a.shape
    d = wd_stack.shape[2]
    return pl.pallas_call(
        _down_kernel,
        grid=(m // tm, d // tn, f // tk),
        in_specs=[
            pl.BlockSpec((tm, tk), lambda i, j, k: (i, k)),
            pl.BlockSpec((None, tk, tn), lambda i, j, k: (layer, k, j)),
        ],
        out_specs=pl.BlockSpec((tm, tn), lambda i, j, k: (i, j)),
        out_shape=jax.ShapeDtypeStruct((m, d), F32),
        compiler_params=_cparams(("arbitrary", "arbitrary", "arbitrary")),
        name="down",
    )(a, wd_stack)


def _gateup_grouped_kernel(te_ref, tr_ref, nu_ref, x_ref, wg_ref, wu_ref, o_ref):
    tm = x_ref.shape[0]
    big_rows = tm - SUB_ROWS
    use_big, first, n_valid = _tile_plan(tr_ref[pl.program_id(0)], tm)

    @pl.when(n_valid > 0)
    def _():
        wg = wg_ref[...].astype(BF16)
        wu = wu_ref[...].astype(BF16)

        @pl.when(use_big)
        def _():
            o_ref[0:big_rows, :] = _swiglu(x_ref[0:big_rows, :], wg, wu)

        def compute(rs):
            o_ref[pl.ds(rs, SUB_ROWS), :] = _swiglu(x_ref[pl.ds(rs, SUB_ROWS), :], wg, wu)

        _for_sub_blocks(first, n_valid, compute)

    def clear(rs):
        o_ref[pl.ds(rs, SUB_ROWS), :] = jnp.zeros((SUB_ROWS, o_ref.shape[1]), o_ref.dtype)

    _for_sub_blocks(jnp.maximum(n_valid, first), tm // SUB_ROWS, clear)


def _gateup_grouped(xs, wg, wu, layer, tile_e, tile_rows, n_used, tm, tn):
    r, k = xs.shape
    f = wg.shape[3]
    nj = f // tn

    def w_map(i, j, te, tr, nu):
        return (layer, te[i], 0, jnp.where(i < nu[0], j, nj - 1))

    w_spec = pl.BlockSpec((None, None, k, tn), w_map)
    return pl.pallas_call(
        _gateup_grouped_kernel,
        grid_spec=pltpu.PrefetchScalarGridSpec(
            num_scalar_prefetch=3,
            grid=(r // tm, nj),
            in_specs=[_row_resident((tm, k), lambda i, j, te, tr, nu: (jnp.minimum(i, nu[0] - 1), 0)),
                      w_spec, w_spec],
            out_specs=pl.BlockSpec((tm, tn), lambda i, j, te, tr, nu: (i, j)),
        ),
        out_shape=jax.ShapeDtypeStruct((r, f), BF16),
        compiler_params=_cparams(("arbitrary", "arbitrary")),
        name="gateup_grouped",
    )(tile_e, tile_rows, n_used, xs, wg, wu)


def _down_grouped_kernel(te_ref, tr_ref, nu_ref, a_ref, w_ref, o_ref):
    tm = a_ref.shape[0]
    use_big, first, n_valid = _tile_plan(tr_ref[pl.program_id(0)], tm)

    @pl.when(pl.program_id(2) == 0)
    def _():
        o_ref[...] = jnp.zeros_like(o_ref)

    @pl.when(n_valid > 0)
    def _():
        w = w_ref[...].astype(BF16)

        @pl.when(use_big)
        def _():
            _down_accumulate(a_ref, w, o_ref, tm - SUB_ROWS)

        def compute(rs):
            o_ref[pl.ds(rs, SUB_ROWS), :] += jnp.dot(a_ref[pl.ds(rs, SUB_ROWS), :], w, preferred_element_type=F32)

        _for_sub_blocks(first, n_valid, compute)


def _down_grouped(a, wd, layer, tile_e, tile_rows, n_used, tm, tn, tk):
    r, f = a.shape
    d = wd.shape[3]
    nk = f // tk
    nj = d // tn

    def a_map(i, j, k, te, tr, nu):
        return (jnp.minimum(i, nu[0] - 1), jnp.where(i < nu[0], k, nk - 1))

    def w_map(i, j, k, te, tr, nu):
        used = i < nu[0]
        return (layer, te[i], jnp.where(used, k, nk - 1), jnp.where(used, j, nj - 1))

    return pl.pallas_call(
        _down_grouped_kernel,
        grid_spec=pltpu.PrefetchScalarGridSpec(
            num_scalar_prefetch=3,
            grid=(r // tm, nj, nk),
            in_specs=[pl.BlockSpec((tm, tk), a_map), pl.BlockSpec((None, None, tk, tn), w_map)],
            out_specs=pl.BlockSpec((tm, tn), lambda i, j, k, te, tr, nu: (i, j)),
        ),
        out_shape=jax.ShapeDtypeStruct((r, d), F32),
        compiler_params=_cparams(("arbitrary", "arbitrary", "arbitrary")),
        name="down_grouped",
    )(tile_e, tile_rows, n_used, a, wd)


def _dispatch_kernel(tr_ref, src_ref, h_hbm, o_ref, buf, sem, *, steps_per_tile):
    rows = o_ref.shape[0]
    step = pl.program_id(0)
    used = (step % steps_per_tile) * rows < tr_ref[step // steps_per_tile]

    def row_copy(r, src_row):
        return pltpu.make_async_copy(h_hbm.at[pl.ds(src_row, 1), :], buf.at[pl.ds(r, 1), :], sem)

    @pl.when(used)
    def _():
        def issue(r, c):
            row_copy(r, src_ref[0, r]).start()
            return c

        def drain(r, c):
            row_copy(r, 0).wait()
            return c

        lax.fori_loop(0, rows, issue, 0)
        lax.fori_loop(0, rows, drain, 0)
        o_ref[...] = buf[...].astype(BF16)

    @pl.when(jnp.logical_not(used))
    def _():
        o_ref[...] = jnp.zeros_like(o_ref)


def _dispatch(h_f32, src, tile_rows, tm, rows_per_step):
    d = h_f32.shape[1]
    r = src.shape[0]
    steps = r // rows_per_step
    return pl.pallas_call(
        functools.partial(_dispatch_kernel, steps_per_tile=tm // rows_per_step),
        grid_spec=pltpu.PrefetchScalarGridSpec(
            num_scalar_prefetch=1,
            grid=(steps,),
            in_specs=[
                pl.BlockSpec((None, 1, rows_per_step), lambda i, tr: (i, 0, 0), memory_space=pltpu.SMEM),
                pl.BlockSpec(memory_space=pl.ANY),
            ],
            out_specs=pl.BlockSpec((rows_per_step, d), lambda i, tr: (i, 0)),
            scratch_shapes=[pltpu.VMEM((rows_per_step, d), F32), pltpu.SemaphoreType.DMA(())],
        ),
        out_shape=jax.ShapeDtypeStruct((r, d), BF16),
        compiler_params=_cparams(("arbitrary",)),
        name="dispatch",
    )(tile_rows, src.reshape(steps, 1, rows_per_step), h_f32)


def _route_meta(ridx, tm, n_tiles):
    m = ridx.shape[0]
    e_flat = ridx[:, :TOP_K].reshape(-1)
    onehot = (e_flat[:, None] == jnp.arange(N_EXPERTS, dtype=jnp.int32)[None, :]).astype(jnp.int32)
    csum = jnp.cumsum(onehot, axis=0)
    rank = jnp.take_along_axis(csum, e_flat[:, None], axis=1)[:, 0] - 1
    counts = csum[-1]
    ptiles = (counts + tm - 1) // tm
    tile_end = jnp.cumsum(ptiles)
    tile_start = tile_end - ptiles
    pos = (tile_start[e_flat] * tm + rank).astype(jnp.int32)
    n_used = tile_end[-1].astype(jnp.int32)
    tiles = jnp.arange(n_tiles, dtype=jnp.int32)
    tile_e_raw = jnp.sum((tiles[:, None] >= tile_end[None, :]).astype(jnp.int32), axis=1)
    last_e = tile_e_raw[jnp.maximum(n_used - 1, 0)]
    tile_e = jnp.minimum(jnp.where(tiles < n_used, tile_e_raw, last_e), N_EXPERTS - 1).astype(jnp.int32)
    tile_rows = jnp.clip(counts[tile_e] - (tiles - tile_start[tile_e]) * tm, 0, tm)
    tile_rows = jnp.where(tiles < n_used, tile_rows, 0).astype(jnp.int32)
    src = jnp.zeros((n_tiles * tm,), jnp.int32).at[pos].set(jnp.arange(m * TOP_K, dtype=jnp.int32) // TOP_K)
    return pos.reshape(m, TOP_K), src, tile_e, tile_rows, n_used.reshape(1)


def _softmax_sink_pv(s, sink, vv):
    m = jnp.maximum(jnp.max(s, axis=1, keepdims=True), sink)
    p = jnp.exp(s - m)
    den = jnp.sum(p, axis=1, keepdims=True) + jnp.exp(sink - m)
    return jnp.dot((p / den).astype(BF16), vv, preferred_element_type=F32)


def _qk(q, kk):
    return lax.dot_general(q, kk, (((1,), (1,)), ((), ())), preferred_element_type=F32) * (1.0 / math.sqrt(HEAD_DIM))


def _attn_prompt_kernel(sink_ref, q_ref, kp_ref, kc_ref, vp_ref, vc_ref, bias_ref, tail_ref, o_ref, *,
                        layer, blocks_per_seq, n_blocks):
    i = pl.program_id(0)

    @pl.when(i < n_blocks)
    def _():
        first = (i % blocks_per_seq) == 0
        col = lax.broadcasted_iota(jnp.int32, (ROW_BLK, 2 * ROW_BLK), 1)
        no_prev = jnp.logical_and(first, col < ROW_BLK)
        for kvh in range(N_KV_HEADS):
            cs = slice(kvh * HEAD_DIM, (kvh + 1) * HEAD_DIM)
            kk = jnp.concatenate([kp_ref[:, cs], kc_ref[:, cs]], axis=0).astype(BF16)
            vv = jnp.concatenate([vp_ref[:, cs], vc_ref[:, cs]], axis=0).astype(BF16)
            for g in range(GQA_GROUP):
                h = kvh * GQA_GROUP + g
                hs = slice(h * HEAD_DIM, (h + 1) * HEAD_DIM)
                s = _qk(q_ref[:, hs].astype(BF16), kk) + bias_ref[h]
                s = jnp.where(no_prev, -jnp.inf, s)
                o_ref[:, hs] = _softmax_sink_pv(s, sink_ref[layer, h], vv).astype(BF16)

    @pl.when(i >= n_blocks)
    def _():
        o_ref[...] = tail_ref[...]


def _attn_prompt(qkv, sinks, bias_p, tail, layer, n_prompt_rows, blocks_per_seq):
    kb = D_ATTN // D_KV
    n_blocks = n_prompt_rows // ROW_BLK

    def prev(i):
        return jnp.maximum(i - 1, 0)

    return pl.pallas_call(
        functools.partial(_attn_prompt_kernel, layer=layer, blocks_per_seq=blocks_per_seq, n_blocks=n_blocks),
        grid=(n_blocks + 1,),
        in_specs=[
            pl.BlockSpec(memory_space=pltpu.SMEM),
            pl.BlockSpec((ROW_BLK, D_ATTN), lambda i: (i, 0)),
            pl.BlockSpec((ROW_BLK, D_KV), lambda i: (prev(i), kb)),
            pl.BlockSpec((ROW_BLK, D_KV), lambda i: (i, kb)),
            pl.BlockSpec((ROW_BLK, D_KV), lambda i: (prev(i), kb + 1)),
            pl.BlockSpec((ROW_BLK, D_KV), lambda i: (i, kb + 1)),
            pl.BlockSpec((N_HEADS, ROW_BLK, 2 * ROW_BLK), lambda i: (0, 0, 0)),
            pl.BlockSpec((ROW_BLK, D_ATTN), lambda i: (0, 0)),
        ],
        out_specs=pl.BlockSpec((ROW_BLK, D_ATTN), lambda i: (i, 0)),
        out_shape=jax.ShapeDtypeStruct((n_prompt_rows + ROW_BLK, D_ATTN), BF16),
        compiler_params=_cparams(("arbitrary",)),
        name="attn_prompt",
    )(sinks, qkv, qkv, qkv, qkv, qkv, bias_p, tail)


def _attn_sample_kernel(q_ref, ck_ref, cv_ref, kn_ref, vn_ref, bias_ref, sink_ref, o_ref):
    for kvh in range(N_KV_HEADS):
        kk = jnp.concatenate([ck_ref[kvh], kn_ref[kvh]], axis=0).astype(BF16)
        vv = jnp.concatenate([cv_ref[kvh], vn_ref[kvh]], axis=0).astype(BF16)
        s = _qk(q_ref[kvh].astype(BF16), kk) + bias_ref[kvh]
        o_ref[kvh] = _softmax_sink_pv(s, sink_ref[kvh], vv)


def _attn_sample(q, ck, cv, kn, vn, bias_s, sink_s):
    b, _, rows, _ = q.shape
    wb = ck.shape[2]
    pad = kn.shape[2]

    def per_b(n):
        return pl.BlockSpec((None, N_KV_HEADS, n, HEAD_DIM), lambda i: (i, 0, 0, 0))

    return pl.pallas_call(
        _attn_sample_kernel,
        grid=(b,),
        in_specs=[
            per_b(rows), per_b(wb), per_b(wb), per_b(pad), per_b(pad),
            pl.BlockSpec((N_KV_HEADS, rows, wb + pad), lambda i: (0, 0, 0)),
            pl.BlockSpec((N_KV_HEADS, rows, 1), lambda i: (0, 0, 0)),
        ],
        out_specs=per_b(rows),
        out_shape=jax.ShapeDtypeStruct(q.shape, F32),
        compiler_params=_cparams(("arbitrary",)),
        name="attn_sample",
    )(q, ck, cv, kn, vn, bias_s, sink_s)


def _conv_pool(ue_ref, ve_ref, conv_base, pool_base, stride, rows, gate_b, pv, cw_ref, pw_ref, ps_ref, cnt_of, o_ref):
    d_conv = gate_b.shape[1]
    y = cw_ref[CONV_W - 1:CONV_W, :] * ue_ref[conv_base:conv_base + rows, :]
    for j in range(1, CONV_W):
        off = conv_base - j * stride
        y = y + cw_ref[CONV_W - 1 - j:CONV_W - j, :] * ue_ref[off:off + rows, :]
    o_ref[:, 0:d_conv] = (gate_b * y).astype(BF16)
    grp = pv.shape[1] // len(POOL_WINDOWS)
    for gi, w in enumerate(POOL_WINDOWS):
        cs = slice(gi * grp, (gi + 1) * grp)
        acc = ve_ref[pool_base:pool_base + rows, cs]
        for j in range(1, w):
            off = pool_base - j * stride
            acc = acc + ve_ref[off:off + rows, cs]
        g = acc / cnt_of(w) - pv[:, cs]
        yg = jnp.dot(g.astype(BF16), pw_ref[gi].astype(BF16), preferred_element_type=F32) * ps_ref[:, cs]
        o_ref[:, d_conv + gi * grp:d_conv + (gi + 1) * grp] = yg.astype(BF16)


CONV_HALO = 8
POOL_HALO = 16


def _convpool_prompt_kernel(gb_ref, gc_ref, xt_ref, pv_ref, gch_ref, xth_ref, pvh_ref, cw_ref, pw_ref, ps_ref,
                            tail_ref, o_ref, ut_ref, ue_ref, ve_ref, *, blocks_per_seq, n_blocks):
    i = pl.program_id(0)

    @pl.when(i < n_blocks)
    def _():
        n = i % blocks_per_seq
        keep = (n != 0).astype(F32)
        u = gc_ref[...] * xt_ref[...]
        ue_ref[0:CONV_HALO, :] = gch_ref[...] * xth_ref[...] * keep
        ue_ref[CONV_HALO:CONV_HALO + ROW_BLK, :] = u
        ut_ref[...] = u[ROW_BLK - CONV_HALO:ROW_BLK, :]
        pv = pv_ref[...]
        ve_ref[0:POOL_HALO, :] = pvh_ref[...] * keep
        ve_ref[POOL_HALO:POOL_HALO + ROW_BLK, :] = pv
        pos1 = (n * ROW_BLK + 1 + lax.broadcasted_iota(jnp.int32, (ROW_BLK, 1), 0)).astype(F32)

        def cnt_of(w):
            return jnp.minimum(pos1, float(w))

        _conv_pool(ue_ref, ve_ref, CONV_HALO, POOL_HALO, 1, ROW_BLK, gb_ref[...], pv, cw_ref, pw_ref, ps_ref,
                   cnt_of, o_ref)

    @pl.when(i >= n_blocks)
    def _():
        o_ref[...] = tail_ref[...]


def _convpool_prompt(rest, tail, conv_w, pool_w, pool_scale3, layer, n_prompt_rows, blocks_per_seq):
    c = rest.shape[1] // 4
    n_blocks = n_prompt_rows // ROW_BLK
    n_seq = n_blocks // blocks_per_seq
    ch, ph = ROW_BLK // CONV_HALO, ROW_BLK // POOL_HALO

    def blk(col):
        return pl.BlockSpec((ROW_BLK, c), lambda i: (i, col))

    def halo(rows, per_blk, col):
        return pl.BlockSpec((rows, c), lambda i: (jnp.maximum(i * per_blk - 1, 0), col))

    return pl.pallas_call(
        functools.partial(_convpool_prompt_kernel, blocks_per_seq=blocks_per_seq, n_blocks=n_blocks),
        grid=(n_blocks + 1,),
        in_specs=[
            blk(0), blk(1), blk(2), blk(3),
            halo(CONV_HALO, ch, 1), halo(CONV_HALO, ch, 2), halo(POOL_HALO, ph, 3),
            pl.BlockSpec((None, CONV_W, c), lambda i: (layer, 0, 0)),
            pl.BlockSpec((None, len(POOL_WINDOWS), c // 4, c // 4), lambda i: (layer, 0, 0, 0)),
            pl.BlockSpec((None, 1, c), lambda i: (layer, 0, 0)),
            pl.BlockSpec((ROW_BLK, 2 * c), lambda i: (0, 0)),
        ],
        out_specs=[
            pl.BlockSpec((ROW_BLK, 2 * c), lambda i: (i, 0)),
            pl.BlockSpec((CONV_HALO, c), lambda i: (jnp.minimum(i // blocks_per_seq, n_seq - 1), 0)),
        ],
        out_shape=[jax.ShapeDtypeStruct((n_prompt_rows + ROW_BLK, 2 * c), BF16),
                   jax.ShapeDtypeStruct((n_seq * CONV_HALO, c), F32)],
        scratch_shapes=[pltpu.VMEM((CONV_HALO + ROW_BLK, c), F32), pltpu.VMEM((POOL_HALO + ROW_BLK, c), F32)],
        compiler_params=_cparams(("arbitrary",)),
        name="convpool_prompt",
    )(rest, rest, rest, rest, rest, rest, rest, conv_w, pool_w, pool_scale3, tail)


def _convpool_sample_kernel(gb_ref, gc_ref, xt_ref, pv_ref, cprev_ref, pprev_ref, cw_ref, pw_ref, ps_ref,
                            o_ref, u_ref, ue_ref, ve_ref, *, stride, pos0):
    rows = gb_ref.shape[0]
    ch, ph = cprev_ref.shape[0], pprev_ref.shape[0]
    u = gc_ref[...] * xt_ref[...]
    u_ref[...] = u
    ue_ref[0:ch, :] = cprev_ref[...]
    ue_ref[ch:ch + rows, :] = u
    pv = pv_ref[...]
    ve_ref[0:ph, :] = pprev_ref[...]
    ve_ref[ph:ph + rows, :] = pv
    pos1 = (pos0 + 1 + lax.broadcasted_iota(jnp.int32, (rows, 1), 0) // stride).astype(F32)

    def cnt_of(w):
        return jnp.minimum(pos1, float(w))

    _conv_pool(ue_ref, ve_ref, ch, ph, stride, rows, gb_ref[...], pv, cw_ref, pw_ref, ps_ref, cnt_of, o_ref)


def _convpool_sample(rest, row_blk_idx, cprev, pprev, conv_w, pool_w, pool_scale3, layer, stride, pos0):
    c = rest.shape[1] // 4
    rows = ROW_BLK

    def blk(col):
        return pl.BlockSpec((rows, c), lambda i: (row_blk_idx, col))

    return pl.pallas_call(
        functools.partial(_convpool_sample_kernel, stride=stride, pos0=pos0),
        grid=(1,),
        in_specs=[
            blk(0), blk(1), blk(2), blk(3),
            pl.BlockSpec(cprev.shape, lambda i: (0, 0)),
            pl.BlockSpec(pprev.shape, lambda i: (0, 0)),
            pl.BlockSpec((None, CONV_W, c), lambda i: (layer, 0, 0)),
            pl.BlockSpec((None, len(POOL_WINDOWS), c // 4, c // 4), lambda i: (layer, 0, 0, 0)),
            pl.BlockSpec((None, 1, c), lambda i: (layer, 0, 0)),
        ],
        out_specs=[pl.BlockSpec((rows, 2 * c), lambda i: (0, 0)), pl.BlockSpec((rows, c), lambda i: (0, 0))],
        out_shape=[jax.ShapeDtypeStruct((rows, 2 * c), BF16), jax.ShapeDtypeStruct((rows, c), F32)],
        scratch_shapes=[pltpu.VMEM((cprev.shape[0] + rows, c), F32), pltpu.VMEM((pprev.shape[0] + rows, c), F32)],
        compiler_params=_cparams(("arbitrary",)),
        name="convpool_sample",
    )(rest, rest, rest, rest, cprev, pprev, conv_w, pool_w, pool_scale3)


TM = 2080
TM_MOE = TM + SUB_ROWS
TN_PROJ = 512
TN_FF = 256
TN_DOWN = 1024
TK_DOWN = 1024
DISPATCH_ROWS = SUB_ROWS
KN_PAD = 16


def kernel(x_prompt, x_sample, cache_k, cache_v, state_conv, state_pool, c_prompt, c_sample, rel_bias, w_ada, b_ada, norm_g, w_in, sinks, conv_w, pool_w, pool_scale, w_out, w_gate_dense, w_up_dense, w_down_dense, w_router, w_gate_moe, w_up_moe, w_down_moe):
    bp, t, d = x_prompt.shape
    bs, s, _ = x_sample.shape
    depth = w_ada.shape[0]
    wb = cache_k.shape[2]
    n_p = bp * t
    n_s = bs * s
    m = n_p + n_s
    assert n_s == ROW_BLK and t % ROW_BLK == 0 and m % TM == 0
    blocks_per_seq = t // ROW_BLK
    d_qkv = D_ATTN + 2 * D_KV
    c_ch = (w_in.shape[2] - d_qkv) // 4

    x = (x_prompt.reshape(n_p, d), x_sample.transpose(1, 0, 2).reshape(n_s, d))

    n_c = bp + bs
    c_rows = -(-n_c // 16) * 16
    c_all = jnp.concatenate([c_prompt, c_sample, jnp.zeros((c_rows - n_c, d), F32)], axis=0)
    mod = _ada_mod(c_all, w_ada, b_ada)

    def mod_rows(l):
        ml = mod[l].reshape(c_rows, N_MOD, d).transpose(1, 0, 2)
        mp = jnp.broadcast_to(ml[:, :bp, None, :], (N_MOD, bp, ROW_BLK, d)).reshape(N_MOD, bp * ROW_BLK, d)
        ms = jnp.tile(ml[:, bp:bp + bs], (1, s, 1))
        return jnp.concatenate([mp, ms], axis=1)

    mods = [mod_rows(l) for l in range(depth)]
    norm_g4 = norm_g.reshape(depth, 4, 1, d)

    qi = jnp.arange(ROW_BLK)[:, None]
    dist_p = ROW_BLK + qi - jnp.arange(2 * ROW_BLK)[None, :]
    bias_p = _bias_table(rel_bias, _masked_bucket(dist_p))
    kn_pad = KN_PAD
    si = jnp.arange(8)[:, None]
    dist_s = si + wb - jnp.arange(wb + kn_pad)[None, :]
    bucket_s = jnp.where((si < s) & (jnp.arange(wb + kn_pad)[None, :] < wb + s), _masked_bucket(dist_s), -1)
    bias_s = _bias_table(rel_bias, bucket_s)[:, :s]
    bias_s = bias_s.reshape(N_KV_HEADS, GQA_GROUP, s, wb + kn_pad).transpose(0, 2, 1, 3)
    bias_s = bias_s.reshape(N_KV_HEADS, s * GQA_GROUP, wb + kn_pad)

    pool_scale3 = pool_scale.reshape(depth, 1, c_ch)
    w_router_pad = jnp.pad(w_router, ((0, 0), (0, 0), (0, LANES - N_EXPERTS)))

    h = _norm_mod(x[0], x[1], norm_g4, 0, mods[0], blocks_per_seq)
    kp_l, vp_l, cp_l, pp_l, ks_l, vs_l, cs_l, ps_l = ([] for _ in range(8))
    for l in range(depth):
        j = l // 2
        is_moe = l % 2 == 1
        qkv = _matmul(h, w_in, l, 0, d_qkv, TM, TN_PROJ)
        rest = _matmul(h, w_in, l, d_qkv // TN_PROJ, 4 * c_ch, TM, TN_PROJ)

        qs = qkv[n_p:, :D_ATTN].reshape(s, bs, N_KV_HEADS, GQA_GROUP, HEAD_DIM)
        qs = qs.transpose(1, 2, 0, 3, 4).reshape(bs, N_KV_HEADS, s * GQA_GROUP, HEAD_DIM)
        k_new = qkv[n_p:, D_ATTN:D_ATTN + D_KV].reshape(s, bs, N_KV_HEADS, HEAD_DIM).transpose(1, 0, 2, 3)
        v_new = qkv[n_p:, D_ATTN + D_KV:].reshape(s, bs, N_KV_HEADS, HEAD_DIM).transpose(1, 0, 2, 3)
        pad_new = lambda a: jnp.pad(a.transpose(0, 2, 1, 3), ((0, 0), (0, 0), (0, kn_pad - s), (0, 0)))
        sink_s = jnp.tile(sinks[l].reshape(N_KV_HEADS, 1, GQA_GROUP), (1, s, 1)).reshape(N_KV_HEADS, s * GQA_GROUP, 1)
        o_s = _attn_sample(qs, cache_k[l].transpose(0, 2, 1, 3), cache_v[l].transpose(0, 2, 1, 3),
                           pad_new(k_new), pad_new(v_new), bias_s, sink_s)
        o_s = o_s.reshape(bs, N_KV_HEADS, s, GQA_GROUP, HEAD_DIM).transpose(2, 0, 1, 3, 4).reshape(n_s, D_ATTN)
        a_all = _attn_prompt(qkv, sinks, bias_p, o_s.astype(BF16), l, n_p, blocks_per_seq)

        cprev = state_conv[l].transpose(1, 0, 2).reshape((CONV_W - 1) * bs, c_ch)
        pprev = state_pool[l].transpose(1, 0, 2).reshape(POOL_PAD * bs, c_ch)
        cp_s, u_s = _convpool_sample(rest, n_p // ROW_BLK, cprev, pprev, conv_w, pool_w, pool_scale3, l, bs, PAST_LEN)
        cp_all, u_tail = _convpool_prompt(rest, cp_s, conv_w, pool_w, pool_scale3, l, n_p, blocks_per_seq)

        def seq_tails(a, n_rows, c0, c1):
            return jnp.stack([a[(b + 1) * t - n_rows:(b + 1) * t, c0:c1] for b in range(bp)])

        kp_l.append(seq_tails(qkv, WINDOW, D_ATTN, D_ATTN + D_KV).reshape(bp, WINDOW, N_KV_HEADS, HEAD_DIM))
        vp_l.append(seq_tails(qkv, WINDOW, D_ATTN + D_KV, d_qkv).reshape(bp, WINDOW, N_KV_HEADS, HEAD_DIM))
        cp_l.append(u_tail.reshape(bp, CONV_HALO, c_ch)[:, CONV_HALO - (CONV_W - 1):])
        pp_l.append(seq_tails(rest, POOL_PAD, 3 * c_ch, 4 * c_ch))
        ks_l.append(jnp.concatenate([cache_k[l], k_new], axis=1)[:, s:])
        vs_l.append(jnp.concatenate([cache_v[l], v_new], axis=1)[:, s:])
        u_bs = u_s.reshape(s, bs, c_ch).transpose(1, 0, 2)
        cs_l.append(jnp.concatenate([state_conv[l], u_bs], axis=1)[:, s:])
        pv_bs = rest[n_p:, 3 * c_ch:].reshape(s, bs, c_ch).transpose(1, 0, 2)
        ps_l.append(jnp.concatenate([state_pool[l], pv_bs], axis=1)[:, s:])

        mo = _matmul2(a_all, cp_all, w_out, l, TM, TN_PROJ)
        nxt2 = (l, 2, mods[l], 4, 3)
        last = l == depth - 1
        nxt_layer = None if last else (l + 1, 0, mods[l + 1], 1, 0)
        split_out = n_p // ROW_BLK if last else 0
        if not is_moe:
            x, h2 = _post(x, mo, norm_g4, l, 1, mods[l], 2, blocks_per_seq, nxt=nxt2)
            act = _gateup(h2, w_gate_dense, w_up_dense, j, TM, TN_FF)
            y = _down(act, w_down_dense, j, TM, TN_DOWN, TK_DOWN)
            outs = _post(x, y, norm_g4, l, 3, mods[l], 5, blocks_per_seq, nxt=nxt_layer, split_out=split_out)
        else:
            x, h2, h2f, ridx, rw = _post(x, mo, norm_g4, l, 1, mods[l], 2, blocks_per_seq, nxt=nxt2,
                                          w_router_pad=w_router_pad[j])
            n_tiles = (m * TOP_K) // TM_MOE + N_EXPERTS
            pos, src, tile_e, tile_rows, n_used = _route_meta(ridx, TM_MOE, n_tiles)
            hs = _dispatch(h2f, src, tile_rows, TM_MOE, DISPATCH_ROWS)
            act = _gateup_grouped(hs, w_gate_moe, w_up_moe, j, tile_e, tile_rows, n_used, TM_MOE, TN_FF)
            ys = _down_grouped(act, w_down_moe, j, tile_e, tile_rows, n_used, TM_MOE, TN_DOWN, TK_DOWN)
            pos_blk = pos.reshape(m // ROW_BLK, ROW_BLK, TOP_K).transpose(0, 2, 1).reshape(m // ROW_BLK, 1, TOP_K * ROW_BLK)
            outs = _post(x, ys, norm_g4, l, 3, mods[l], 5, blocks_per_seq, nxt=nxt_layer, gather=(pos_blk, rw),
                         split_out=split_out)
        if last:
            x = tuple(outs)
        else:
            x, h = outs

    y_prompt = x[0].reshape(bp, t, d)
    y_sample = x[1].reshape(s, bs, d).transpose(1, 0, 2)
    st = jnp.stack
    return (y_prompt, y_sample, st(kp_l), st(vp_l), st(cp_l), st(pp_l), st(ks_l), st(vs_l), st(cs_l), st(ps_l))
```

```python
import functools
import math

import jax
import jax.numpy as jnp
from jax import lax
from jax.experimental import pallas as pl
from jax.experimental.pallas import tpu as pltpu

F32 = jnp.float32
BF16 = jnp.bfloat16

N_HEADS = 32
N_KV_HEADS = 4
HEAD_DIM = 64
GQA_GROUP = N_HEADS // N_KV_HEADS
D_ATTN = N_HEADS * HEAD_DIM
D_KV = N_KV_HEADS * HEAD_DIM
WINDOW = 128
N_BUCKETS = 32
MAX_EXACT = N_BUCKETS // 2
MAX_DISTANCE = WINDOW
CONV_W = 3
POOL_WINDOWS = (2, 4, 8, 16)
POOL_PAD = max(POOL_WINDOWS) - 1
N_EXPERTS = 8
TOP_K = 2
N_MOD = 6
RMS_EPS = 1e-6
PAST_LEN = 16384

ROW_BLK = 128
LANES = 128
VMEM_LIMIT = 56 * 1024 * 1024


def _cparams(sem):
    return pltpu.CompilerParams(dimension_semantics=sem, vmem_limit_bytes=VMEM_LIMIT)


def _rms(x, g):
    return x * lax.rsqrt(jnp.mean(x * x, axis=-1, keepdims=True) + RMS_EPS) * g


def _silu(x):
    return x / (1.0 + jnp.exp(-x))


def _ada_kernel(c_ref, w_ref, b_ref, o_ref):
    s = _silu(c_ref[...]).astype(BF16)
    o_ref[...] = jnp.dot(s, w_ref[...].astype(BF16), preferred_element_type=F32) + b_ref[...]


def _ada_mod(c_pad, w_ada, b_ada, tn=512):
    depth, d, n = w_ada.shape
    rows = c_pad.shape[0]
    return pl.pallas_call(
        _ada_kernel,
        grid=(depth, n // tn),
        in_specs=[
            pl.BlockSpec((rows, d), lambda l, j: (0, 0)),
            pl.BlockSpec((None, d, tn), lambda l, j: (l, 0, j)),
            pl.BlockSpec((None, 1, tn), lambda l, j: (l, 0, j)),
        ],
        out_specs=pl.BlockSpec((None, rows, tn), lambda l, j: (l, 0, j)),
        out_shape=jax.ShapeDtypeStruct((depth, rows, n), F32),
        compiler_params=_cparams(("arbitrary", "arbitrary")),
        name="ada_mod",
    )(c_pad, w_ada, b_ada.reshape(depth, 1, n))


def _bias_kernel(rb_ref, bucket_ref, o_ref):
    h = pl.program_id(0)
    bucket = bucket_ref[...]
    acc = jnp.full(bucket.shape, -jnp.inf, F32)
    for b in range(N_BUCKETS):
        acc = jnp.where(bucket == b, rb_ref[b, h], acc)
    o_ref[...] = acc


def _bias_table(rel_bias, bucket):
    rq, rk = bucket.shape
    return pl.pallas_call(
        _bias_kernel,
        grid=(N_HEADS,),
        in_specs=[
            pl.BlockSpec(memory_space=pltpu.SMEM),
            pl.BlockSpec((rq, rk), lambda h: (0, 0)),
        ],
        out_specs=pl.BlockSpec((None, rq, rk), lambda h: (h, 0, 0)),
        out_shape=jax.ShapeDtypeStruct((N_HEADS, rq, rk), F32),
        compiler_params=_cparams(("arbitrary",)),
        name="bias_table",
    )(rel_bias, bucket)


def _t5_bucket(dist):
    n = jnp.maximum(dist, 0)
    nf = jnp.maximum(n, 1).astype(F32)
    large = MAX_EXACT + (jnp.log(nf / MAX_EXACT) / math.log(MAX_DISTANCE / MAX_EXACT)
                         * (N_BUCKETS - MAX_EXACT)).astype(jnp.int32)
    large = jnp.minimum(large, N_BUCKETS - 1)
    return jnp.where(n < MAX_EXACT, n, large)


def _masked_bucket(dist):
    return jnp.where((dist >= 0) & (dist < WINDOW), _t5_bucket(dist), -1).astype(jnp.int32)


def _mod_spec(which, blocks_per_group, n_groups, d):
    last = n_groups - 1
    return pl.BlockSpec((None, ROW_BLK, d),
                        lambda i: (which, jnp.minimum(i // blocks_per_group, last), 0))


def _split_specs(n_lead_blocks, d):
    lead = pl.BlockSpec((ROW_BLK, d), lambda i: (jnp.minimum(i, n_lead_blocks - 1), 0))
    tail = pl.BlockSpec((ROW_BLK, d), lambda i: (0, 0))
    return [lead, tail]


def _read_split(lead_ref, tail_ref, n_lead_blocks):
    return jnp.where(pl.program_id(0) < n_lead_blocks, lead_ref[...], tail_ref[...])


def _norm_mod_kernel(xp_ref, xs_ref, g_ref, sc_ref, sh_ref, h_ref, *, n_lead_blocks):
    x = _read_split(xp_ref, xs_ref, n_lead_blocks)
    h = _rms(x, g_ref[...]) * (1.0 + sc_ref[...]) + sh_ref[...]
    h_ref[...] = h.astype(BF16)


def _norm_mod(x_lead, x_tail, norm_g4, layer, mods, blocks_per_group):
    n_lead, d = x_lead.shape
    m = n_lead + x_tail.shape[0]
    n_groups = mods.shape[1] // ROW_BLK
    return pl.pallas_call(
        functools.partial(_norm_mod_kernel, n_lead_blocks=n_lead // ROW_BLK),
        grid=(m // ROW_BLK,),
        in_specs=_split_specs(n_lead // ROW_BLK, d) + [
            pl.BlockSpec((None, None, 1, d), lambda i: (layer, 0, 0, 0)),
            _mod_spec(1, blocks_per_group, n_groups, d),
            _mod_spec(0, blocks_per_group, n_groups, d),
        ],
        out_specs=pl.BlockSpec((ROW_BLK, d), lambda i: (i, 0)),
        out_shape=jax.ShapeDtypeStruct((m, d), BF16),
        compiler_params=_cparams(("arbitrary",)),
        name="norm_mod",
    )(x_lead, x_tail, norm_g4, mods, mods)


def _top2(logits):
    lane = lax.broadcasted_iota(jnp.int32, logits.shape, 1).astype(F32)
    lg = jnp.where(lane < N_EXPERTS, logits, -jnp.inf)
    m1 = jnp.max(lg, axis=1, keepdims=True)
    i1 = jnp.min(jnp.where(lg == m1, lane, float(LANES)), axis=1, keepdims=True)
    lg2 = jnp.where(lane == i1, -jnp.inf, lg)
    m2 = jnp.max(lg2, axis=1, keepdims=True)
    i2 = jnp.min(jnp.where(lg2 == m2, lane, float(LANES)), axis=1, keepdims=True)
    e2 = jnp.exp(m2 - m1)
    w1 = 1.0 / (1.0 + e2)
    w2 = e2 / (1.0 + e2)
    ridx = jnp.where(lane == 0, i1, jnp.where(lane == 1, i2, 0.0)).astype(jnp.int32)
    rw = jnp.where(lane == 0, w1, jnp.where(lane == 1, w2, 0.0))
    return ridx, rw


def _post_kernel(*refs, has_next, route, gather, split_in, split_out):
    it = iter(refs)
    if gather:
        pos_ref, rw_ref, ys_hbm = next(it), next(it), next(it)
    else:
        y_ref = next(it)
    x_ref = next(it)
    if split_in:
        xs_ref = next(it)
    ga_ref, gate_ref = next(it), next(it)
    if has_next:
        gb_ref, sc_ref, sh_ref = next(it), next(it), next(it)
    if route:
        wr_ref = next(it)
    xo_ref = next(it)
    if split_out:
        xos_ref = next(it)
    if has_next:
        h_ref = next(it)
    if route:
        hf_ref, ridx_ref, rwo_ref = next(it), next(it), next(it)
    if gather:
        buf, sem = next(it), next(it)

    if gather:
        n_rows = TOP_K * ROW_BLK

        def row_copy(r, src_row):
            return pltpu.make_async_copy(ys_hbm.at[pl.ds(src_row, 1), :], buf.at[pl.ds(r, 1), :], sem)

        def issue(r, c):
            row_copy(r, pos_ref[0, r]).start()
            return c

        def drain(r, c):
            row_copy(r, 0).wait()
            return c

        lax.fori_loop(0, n_rows, issue, 0)
        lax.fori_loop(0, n_rows, drain, 0)
        rw = rw_ref[...]
        y = rw[:, 0:1] * buf[0:ROW_BLK, :] + rw[:, 1:2] * buf[ROW_BLK:n_rows, :]
    else:
        y = y_ref[...]

    x_in = _read_split(x_ref, xs_ref, split_in) if split_in else x_ref[...]
    x = x_in + gate_ref[...] * _rms(y, ga_ref[...])
    if split_out:
        @pl.when(pl.program_id(0) < split_out)
        def _():
            xo_ref[...] = x

        @pl.when(pl.program_id(0) >= split_out)
        def _():
            xos_ref[...] = x
    else:
        xo_ref[...] = x
    if has_next:
        h = _rms(x, gb_ref[...]) * (1.0 + sc_ref[...]) + sh_ref[...]
        h_ref[...] = h.astype(BF16)
        if route:
            hf_ref[...] = h
            logits = jnp.dot(h.astype(BF16), wr_ref[...].astype(BF16), preferred_element_type=F32)
            ridx, rw_out = _top2(logits)
            ridx_ref[...] = ridx
            rwo_ref[...] = rw_out


def _post(x, y, norm_g4, layer, ga_idx, mods, gate_idx, blocks_per_group, *,
          nxt=None, w_router_pad=None, gather=None, split_out=0):
    split_in = 0
    if isinstance(x, tuple):
        x_lead, x_tail = x
        split_in = x_lead.shape[0] // ROW_BLK
        m, d = x_lead.shape[0] + x_tail.shape[0], x_lead.shape[1]
    else:
        m, d = x.shape
    n_groups = mods.shape[1] // ROW_BLK
    row = pl.BlockSpec((ROW_BLK, d), lambda i: (i, 0))
    lane_row = pl.BlockSpec((ROW_BLK, LANES), lambda i: (i, 0))
    has_next = nxt is not None
    route = w_router_pad is not None
    args, specs, scratch = [], [], []
    if gather is not None:
        pos, rw = gather
        args += [pos, rw, y]
        specs += [pl.BlockSpec((None, 1, TOP_K * ROW_BLK), lambda i: (i, 0, 0), memory_space=pltpu.SMEM),
                  lane_row, pl.BlockSpec(memory_space=pl.ANY)]
        scratch = [pltpu.VMEM((TOP_K * ROW_BLK, d), F32), pltpu.SemaphoreType.DMA(())]
    else:
        args.append(y)
        specs.append(row)
    if split_in:
        args += [x_lead, x_tail]
        specs += _split_specs(split_in, d)
    else:
        args.append(x)
        specs.append(row)
    args += [norm_g4, mods]
    specs += [pl.BlockSpec((None, None, 1, d), lambda i: (layer, ga_idx, 0, 0)),
              _mod_spec(gate_idx, blocks_per_group, n_groups, d)]
    if split_out:
        out_shape = [jax.ShapeDtypeStruct((split_out * ROW_BLK, d), F32),
                     jax.ShapeDtypeStruct((m - split_out * ROW_BLK, d), F32)]
        out_specs = _split_specs(split_out, d)
    else:
        out_shape = [jax.ShapeDtypeStruct((m, d), F32)]
        out_specs = [row]
    if has_next:
        n_layer, n_gidx, n_mods, sc_idx, sh_idx = nxt
        args += [norm_g4, n_mods, n_mods]
        specs += [pl.BlockSpec((None, None, 1, d), lambda i: (n_layer, n_gidx, 0, 0)),
                  _mod_spec(sc_idx, blocks_per_group, n_groups, d),
                  _mod_spec(sh_idx, blocks_per_group, n_groups, d)]
        out_shape.append(jax.ShapeDtypeStruct((m, d), BF16))
        out_specs.append(row)
    if route:
        args.append(w_router_pad)
        specs.append(pl.BlockSpec((d, LANES), lambda i: (0, 0)))
        out_shape += [jax.ShapeDtypeStruct((m, d), F32), jax.ShapeDtypeStruct((m, LANES), jnp.int32),
                      jax.ShapeDtypeStruct((m, LANES), F32)]
        out_specs += [row, lane_row, lane_row]
    return pl.pallas_call(
        functools.partial(_post_kernel, has_next=has_next, route=route, gather=gather is not None,
                          split_in=split_in, split_out=split_out),
        grid=(m // ROW_BLK,),
        in_specs=specs,
        out_specs=out_specs,
        out_shape=out_shape,
        scratch_shapes=scratch,
        compiler_params=_cparams(("arbitrary",)),
        name="post_gather" if gather is not None else ("post_route" if route else "post"),
    )(*args)


def _row_resident(shape, index_map):
    return pl.BlockSpec(shape, index_map, pipeline_mode=pl.Buffered(1))


def _mm_kernel(x_ref, w_ref, o_ref):
    o_ref[...] = jnp.dot(x_ref[...], w_ref[...].astype(BF16), preferred_element_type=F32).astype(o_ref.dtype)


def _matmul(x, w_stack, layer, col_blk0, n_out, tm, tn):
    m, k = x.shape
    return pl.pallas_call(
        _mm_kernel,
        grid=(m // tm, n_out // tn),
        in_specs=[
            _row_resident((tm, k), lambda i, j: (i, 0)),
            pl.BlockSpec((None, k, tn), lambda i, j: (layer, 0, col_blk0 + j)),
        ],
        out_specs=pl.BlockSpec((tm, tn), lambda i, j: (i, j)),
        out_shape=jax.ShapeDtypeStruct((m, n_out), F32),
        compiler_params=_cparams(("arbitrary", "arbitrary")),
        name="matmul",
    )(x, w_stack)


def _mm2_kernel(a_ref, c_ref, w_ref, o_ref):
    ka = a_ref.shape[1]
    acc = jnp.dot(a_ref[...], w_ref[0:ka, :].astype(BF16), preferred_element_type=F32)
    acc = acc + jnp.dot(c_ref[...], w_ref[ka:, :].astype(BF16), preferred_element_type=F32)
    o_ref[...] = acc


def _matmul2(a, c, w_stack, layer, tm, tn):
    m, ka = a.shape
    kc = c.shape[1]
    n = w_stack.shape[2]
    return pl.pallas_call(
        _mm2_kernel,
        grid=(m // tm, n // tn),
        in_specs=[
            _row_resident((tm, ka), lambda i, j: (i, 0)),
            _row_resident((tm, kc), lambda i, j: (i, 0)),
            pl.BlockSpec((None, ka + kc, tn), lambda i, j: (layer, 0, j)),
        ],
        out_specs=pl.BlockSpec((tm, tn), lambda i, j: (i, j)),
        out_shape=jax.ShapeDtypeStruct((m, n), F32),
        compiler_params=_cparams(("arbitrary", "arbitrary")),
        name="matmul_out",
    )(a, c, w_stack)


SUB_ROWS = 208


def _swiglu(x, wg, wu):
    g = jnp.dot(x, wg, preferred_element_type=F32)
    u = jnp.dot(x, wu, preferred_element_type=F32)
    return (_silu(g) * u).astype(BF16)


def _gateup_body(x_ref, wg_ref, wu_ref, o_ref):
    o_ref[...] = _swiglu(x_ref[...], wg_ref[...].astype(BF16), wu_ref[...].astype(BF16))


MED_ROWS = 1040


def _tile_plan(rows, tm):
    n_big_sub = tm // SUB_ROWS - 1
    n_med_sub = MED_ROWS // SUB_ROWS
    n_valid = (rows + SUB_ROWS - 1) // SUB_ROWS
    use_big = n_valid >= n_big_sub
    use_med = jnp.logical_and(jnp.logical_not(use_big), n_valid >= n_med_sub)
    first = jnp.where(use_big, n_big_sub, jnp.where(use_med, n_med_sub, 0))
    return use_big, use_med, first, n_valid


def _for_sub_blocks(lo, hi, fn):
    def body(r, c):
        fn(pl.multiple_of(r * SUB_ROWS, SUB_ROWS))
        return c

    lax.fori_loop(lo, hi, body, 0)


def _gateup_kernel(x_ref, wg_ref, wu_ref, o_ref):
    _gateup_body(x_ref, wg_ref, wu_ref, o_ref)


def _gateup(x, wg_stack, wu_stack, layer, tm, tn):
    m, k = x.shape
    f = wg_stack.shape[2]
    w_spec = pl.BlockSpec((None, k, tn), lambda i, j: (layer, 0, j))
    return pl.pallas_call(
        _gateup_kernel,
        grid=(m // tm, f // tn),
        in_specs=[_row_resident((tm, k), lambda i, j: (i, 0)), w_spec, w_spec],
        out_specs=pl.BlockSpec((tm, tn), lambda i, j: (i, j)),
        out_shape=jax.ShapeDtypeStruct((m, f), BF16),
        compiler_params=_cparams(("arbitrary", "arbitrary")),
        name="gateup",
    )(x, wg_stack, wu_stack)


ACC_ROWS = 1040
ACC_COLS = 256


def _down_accumulate(a_ref, w, o_ref, n_rows):
    tn = o_ref.shape[1]
    for c0 in range(0, tn, ACC_COLS):
        wc = w[:, c0:c0 + ACC_COLS]
        for r0 in range(0, n_rows, ACC_ROWS):
            r1 = min(r0 + ACC_ROWS, n_rows)
            o_ref[r0:r1, c0:c0 + ACC_COLS] += jnp.dot(a_ref[r0:r1, :], wc, preferred_element_type=F32)


def _down_kernel(a_ref, w_ref, o_ref):
    @pl.when(pl.program_id(2) == 0)
    def _():
        o_ref[...] = jnp.zeros_like(o_ref)

    _down_accumulate(a_ref, w_ref[...].astype(BF16), o_ref, o_ref.shape[0])


def _down(a, wd_stack, layer, tm, tn, tk):
    m, f = a.shape
    d = wd_stack.shape[2]
    return pl.pallas_call(
        _down_kernel,
        grid=(m // tm, d // tn, f // tk),
        in_specs=[
            pl.BlockSpec((tm, tk), lambda i, j, k: (i, k)),
            pl.BlockSpec((None, tk, tn), lambda i, j, k: (layer, k, j)),
        ],
        out_specs=pl.BlockSpec((tm, tn), lambda i, j, k: (i, j)),
        out_shape=jax.ShapeDtypeStruct((m, d), F32),
        compiler_params=_cparams(("arbitrary", "arbitrary", "arbitrary")),
        name="down",
    )(a, wd_stack)


def _gateup_grouped_kernel(te_ref, tr_ref, nu_ref, x_ref, wg_ref, wu_ref, o_ref):
    tm = x_ref.shape[0]
    big_rows = tm - SUB_ROWS
    use_big, use_med, first, n_valid = _tile_plan(tr_ref[pl.program_id(0)], tm)

    @pl.when(n_valid > 0)
    def _():
        wg = wg_ref[...].astype(BF16)
        wu = wu_ref[...].astype(BF16)

        @pl.when(use_big)
        def _():
            o_ref[0:big_rows, :] = _swiglu(x_ref[0:big_rows, :], wg, wu)

        @pl.when(use_med)
        def _():
            o_ref[0:MED_ROWS, :] = _swiglu(x_ref[0:MED_ROWS, :], wg, wu)

        def compute(rs):
            o_ref[pl.ds(rs, SUB_ROWS), :] = _swiglu(x_ref[pl.ds(rs, SUB_ROWS), :], wg, wu)

        _for_sub_blocks(first, n_valid, compute)

    def clear(rs):
        o_ref[pl.ds(rs, SUB_ROWS), :] = jnp.zeros((SUB_ROWS, o_ref.shape[1]), o_ref.dtype)

    _for_sub_blocks(jnp.maximum(n_valid, first), tm // SUB_ROWS, clear)


def _gateup_grouped(xs, wg, wu, layer, tile_e, tile_rows, n_used, tm, tn):
    r, k = xs.shape
    f = wg.shape[3]
    nj = f // tn

    def w_map(i, j, te, tr, nu):
        return (layer, te[i], 0, jnp.where(i < nu[0], j, nj - 1))

    w_spec = pl.BlockSpec((None, None, k, tn), w_map)
    return pl.pallas_call(
        _gateup_grouped_kernel,
        grid_spec=pltpu.PrefetchScalarGridSpec(
            num_scalar_prefetch=3,
            grid=(r // tm, nj),
            in_specs=[_row_resident((tm, k), lambda i, j, te, tr, nu: (jnp.minimum(i, nu[0] - 1), 0)),
                      w_spec, w_spec],
            out_specs=pl.BlockSpec((tm, tn), lambda i, j, te, tr, nu: (i, j)),
        ),
        out_shape=jax.ShapeDtypeStruct((r, f), BF16),
        compiler_params=_cparams(("arbitrary", "arbitrary")),
        name="gateup_grouped",
    )(tile_e, tile_rows, n_used, xs, wg, wu)


def _down_grouped_kernel(te_ref, tr_ref, nu_ref, a_ref, w_ref, o_ref):
    tm = a_ref.shape[0]
    use_big, use_med, first, n_valid = _tile_plan(tr_ref[pl.program_id(0)], tm)

    @pl.when(pl.program_id(2) == 0)
    def _():
        o_ref[...] = jnp.zeros_like(o_ref)

    @pl.when(n_valid > 0)
    def _():
        w = w_ref[...].astype(BF16)

        @pl.when(use_big)
        def _():
            _down_accumulate(a_ref, w, o_ref, tm - SUB_ROWS)

        @pl.when(use_med)
        def _():
            _down_accumulate(a_ref, w, o_ref, MED_ROWS)

        def compute(rs):
            o_ref[pl.ds(rs, SUB_ROWS), :] += jnp.dot(a_ref[pl.ds(rs, SUB_ROWS), :], w, preferred_element_type=F32)

        _for_sub_blocks(first, n_valid, compute)


def _down_grouped(a, wd, layer, tile_e, tile_rows, n_used, tm, tn, tk):
    r, f = a.shape
    d = wd.shape[3]
    nk = f // tk
    nj = d // tn

    def a_map(i, j, k, te, tr, nu):
        return (jnp.minimum(i, nu[0] - 1), jnp.where(i < nu[0], k, nk - 1))

    def w_map(i, j, k, te, tr, nu):
        used = i < nu[0]
        return (layer, te[i], jnp.where(used, k, nk - 1), jnp.where(used, j, nj - 1))

    return pl.pallas_call(
        _down_grouped_kernel,
        grid_spec=pltpu.PrefetchScalarGridSpec(
            num_scalar_prefetch=3,
            grid=(r // tm, nj, nk),
            in_specs=[pl.BlockSpec((tm, tk), a_map), pl.BlockSpec((None, None, tk, tn), w_map)],
            out_specs=pl.BlockSpec((tm, tn), lambda i, j, k, te, tr, nu: (i, j)),
        ),
        out_shape=jax.ShapeDtypeStruct((r, d), F32),
        compiler_params=_cparams(("arbitrary", "arbitrary", "arbitrary")),
        name="down_grouped",
    )(tile_e, tile_rows, n_used, a, wd)


def _dispatch_kernel(tr_ref, src_ref, h_hbm, o_ref, buf, sem, *, steps_per_tile):
    rows = o_ref.shape[0]
    step = pl.program_id(0)
    used = (step % steps_per_tile) * rows < tr_ref[step // steps_per_tile]

    def row_copy(r, src_row):
        return pltpu.make_async_copy(h_hbm.at[pl.ds(src_row, 1), :], buf.at[pl.ds(r, 1), :], sem)

    @pl.when(used)
    def _():
        def issue(r, c):
            row_copy(r, src_ref[0, r]).start()
            return c

        def drain(r, c):
            row_copy(r, 0).wait()
            return c

        lax.fori_loop(0, rows, issue, 0)
        lax.fori_loop(0, rows, drain, 0)
        o_ref[...] = buf[...].astype(BF16)

    @pl.when(jnp.logical_not(used))
    def _():
        o_ref[...] = jnp.zeros_like(o_ref)


def _dispatch(h_f32, src, tile_rows, tm, rows_per_step):
    d = h_f32.shape[1]
    r = src.shape[0]
    steps = r // rows_per_step
    return pl.pallas_call(
        functools.partial(_dispatch_kernel, steps_per_tile=tm // rows_per_step),
        grid_spec=pltpu.PrefetchScalarGridSpec(
            num_scalar_prefetch=1,
            grid=(steps,),
            in_specs=[
                pl.BlockSpec((None, 1, rows_per_step), lambda i, tr: (i, 0, 0), memory_space=pltpu.SMEM),
                pl.BlockSpec(memory_space=pl.ANY),
            ],
            out_specs=pl.BlockSpec((rows_per_step, d), lambda i, tr: (i, 0)),
            scratch_shapes=[pltpu.VMEM((rows_per_step, d), F32), pltpu.SemaphoreType.DMA(())],
        ),
        out_shape=jax.ShapeDtypeStruct((r, d), BF16),
        compiler_params=_cparams(("arbitrary",)),
        name="dispatch",
    )(tile_rows, src.reshape(steps, 1, rows_per_step), h_f32)


def _route_meta(ridx, tm, n_tiles):
    m = ridx.shape[0]
    e_flat = ridx[:, :TOP_K].reshape(-1)
    onehot = (e_flat[:, None] == jnp.arange(N_EXPERTS, dtype=jnp.int32)[None, :]).astype(jnp.int32)
    csum = jnp.cumsum(onehot, axis=0)
    rank = jnp.take_along_axis(csum, e_flat[:, None], axis=1)[:, 0] - 1
    counts = csum[-1]
    ptiles = (counts + tm - 1) // tm
    tile_end = jnp.cumsum(ptiles)
    tile_start = tile_end - ptiles
    pos = (tile_start[e_flat] * tm + rank).astype(jnp.int32)
    n_used = tile_end[-1].astype(jnp.int32)
    tiles = jnp.arange(n_tiles, dtype=jnp.int32)
    tile_e_raw = jnp.sum((tiles[:, None] >= tile_end[None, :]).astype(jnp.int32), axis=1)
    last_e = tile_e_raw[jnp.maximum(n_used - 1, 0)]
    tile_e = jnp.minimum(jnp.where(tiles < n_used, tile_e_raw, last_e), N_EXPERTS - 1).astype(jnp.int32)
    tile_rows = jnp.clip(counts[tile_e] - (tiles - tile_start[tile_e]) * tm, 0, tm)
    tile_rows = jnp.where(tiles < n_used, tile_rows, 0).astype(jnp.int32)
    src = jnp.zeros((n_tiles * tm,), jnp.int32).at[pos].set(jnp.arange(m * TOP_K, dtype=jnp.int32) // TOP_K)
    return pos.reshape(m, TOP_K), src, tile_e, tile_rows, n_used.reshape(1)


def _softmax_sink_pv(s, sink, vv):
    m = jnp.maximum(jnp.max(s, axis=1, keepdims=True), sink)
    p = jnp.exp(s - m)
    den = jnp.sum(p, axis=1, keepdims=True) + jnp.exp(sink - m)
    return jnp.dot((p / den).astype(BF16), vv, preferred_element_type=F32)


def _qk(q, kk):
    return lax.dot_general(q, kk, (((1,), (1,)), ((), ())), preferred_element_type=F32) * (1.0 / math.sqrt(HEAD_DIM))


def _attn_prompt_kernel(sink_ref, q_ref, kp_ref, kc_ref, vp_ref, vc_ref, bias_ref, tail_ref, o_ref, *,
                        layer, blocks_per_seq, n_blocks):
    i = pl.program_id(0)

    @pl.when(i < n_blocks)
    def _():
        first = (i % blocks_per_seq) == 0
        col = lax.broadcasted_iota(jnp.int32, (ROW_BLK, 2 * ROW_BLK), 1)
        no_prev = jnp.logical_and(first, col < ROW_BLK)
        for kvh in range(N_KV_HEADS):
            cs = slice(kvh * HEAD_DIM, (kvh + 1) * HEAD_DIM)
            kk = jnp.concatenate([kp_ref[:, cs], kc_ref[:, cs]], axis=0).astype(BF16)
            vv = jnp.concatenate([vp_ref[:, cs], vc_ref[:, cs]], axis=0).astype(BF16)
            for g in range(GQA_GROUP):
                h = kvh * GQA_GROUP + g
                hs = slice(h * HEAD_DIM, (h + 1) * HEAD_DIM)
                s = _qk(q_ref[:, hs].astype(BF16), kk) + bias_ref[h]
                s = jnp.where(no_prev, -jnp.inf, s)
                o_ref[:, hs] = _softmax_sink_pv(s, sink_ref[layer, h], vv).astype(BF16)

    @pl.when(i >= n_blocks)
    def _():
        o_ref[...] = tail_ref[...]


def _attn_prompt(qkv, sinks, bias_p, tail, layer, n_prompt_rows, blocks_per_seq):
    kb = D_ATTN // D_KV
    n_blocks = n_prompt_rows // ROW_BLK

    def prev(i):
        return jnp.maximum(i - 1, 0)

    return pl.pallas_call(
        functools.partial(_attn_prompt_kernel, layer=layer, blocks_per_seq=blocks_per_seq, n_blocks=n_blocks),
        grid=(n_blocks + 1,),
        in_specs=[
            pl.BlockSpec(memory_space=pltpu.SMEM),
            pl.BlockSpec((ROW_BLK, D_ATTN), lambda i: (i, 0)),
            pl.BlockSpec((ROW_BLK, D_KV), lambda i: (prev(i), kb)),
            pl.BlockSpec((ROW_BLK, D_KV), lambda i: (i, kb)),
            pl.BlockSpec((ROW_BLK, D_KV), lambda i: (prev(i), kb + 1)),
            pl.BlockSpec((ROW_BLK, D_KV), lambda i: (i, kb + 1)),
            pl.BlockSpec((N_HEADS, ROW_BLK, 2 * ROW_BLK), lambda i: (0, 0, 0)),
            pl.BlockSpec((ROW_BLK, D_ATTN), lambda i: (0, 0)),
        ],
        out_specs=pl.BlockSpec((ROW_BLK, D_ATTN), lambda i: (i, 0)),
        out_shape=jax.ShapeDtypeStruct((n_prompt_rows + ROW_BLK, D_ATTN), BF16),
        compiler_params=_cparams(("arbitrary",)),
        name="attn_prompt",
    )(sinks, qkv, qkv, qkv, qkv, qkv, bias_p, tail)


def _attn_sample_kernel(q_ref, ck_ref, cv_ref, kn_ref, vn_ref, bias_ref, sink_ref, o_ref):
    for kvh in range(N_KV_HEADS):
        kk = jnp.concatenate([ck_ref[kvh], kn_ref[kvh]], axis=0).astype(BF16)
        vv = jnp.concatenate([cv_ref[kvh], vn_ref[kvh]], axis=0).astype(BF16)
        s = _qk(q_ref[kvh].astype(BF16), kk) + bias_ref[kvh]
        o_ref[kvh] = _softmax_sink_pv(s, sink_ref[kvh], vv)


def _attn_sample(q, ck, cv, kn, vn, bias_s, sink_s):
    b, _, rows, _ = q.shape
    wb = ck.shape[2]
    pad = kn.shape[2]

    def per_b(n):
        return pl.BlockSpec((None, N_KV_HEADS, n, HEAD_DIM), lambda i: (i, 0, 0, 0))

    return pl.pallas_call(
        _attn_sample_kernel,
        grid=(b,),
        in_specs=[
            per_b(rows), per_b(wb), per_b(wb), per_b(pad), per_b(pad),
            pl.BlockSpec((N_KV_HEADS, rows, wb + pad), lambda i: (0, 0, 0)),
            pl.BlockSpec((N_KV_HEADS, rows, 1), lambda i: (0, 0, 0)),
        ],
        out_specs=per_b(rows),
        out_shape=jax.ShapeDtypeStruct(q.shape, F32),
        compiler_params=_cparams(("arbitrary",)),
        name="attn_sample",
    )(q, ck, cv, kn, vn, bias_s, sink_s)


def _conv_pool(ue_ref, ve_ref, conv_base, pool_base, stride, rows, gate_b, pv, cw_ref, pw_ref, ps_ref, cnt_of, o_ref):
    d_conv = gate_b.shape[1]
    y = cw_ref[CONV_W - 1:CONV_W, :] * ue_ref[conv_base:conv_base + rows, :]
    for j in range(1, CONV_W):
        off = conv_base - j * stride
        y = y + cw_ref[CONV_W - 1 - j:CONV_W - j, :] * ue_ref[off:off + rows, :]
    o_ref[:, 0:d_conv] = (gate_b * y).astype(BF16)
    grp = pv.shape[1] // len(POOL_WINDOWS)
    for gi, w in enumerate(POOL_WINDOWS):
        cs = slice(gi * grp, (gi + 1) * grp)
        acc = ve_ref[pool_base:pool_base + rows, cs]
        for j in range(1, w):
            off = pool_base - j * stride
            acc = acc + ve_ref[off:off + rows, cs]
        g = acc / cnt_of(w) - pv[:, cs]
        yg = jnp.dot(g.astype(BF16), pw_ref[gi].astype(BF16), preferred_element_type=F32) * ps_ref[:, cs]
        o_ref[:, d_conv + gi * grp:d_conv + (gi + 1) * grp] = yg.astype(BF16)


CONV_HALO = 8
POOL_HALO = 16


def _convpool_prompt_kernel(gb_ref, gc_ref, xt_ref, pv_ref, gch_ref, xth_ref, pvh_ref, cw_ref, pw_ref, ps_ref,
                            tail_ref, o_ref, ut_ref, ue_ref, ve_ref, *, blocks_per_seq, n_blocks):
    i = pl.program_id(0)

    @pl.when(i < n_blocks)
    def _():
        n = i % blocks_per_seq
        keep = (n != 0).astype(F32)
        u = gc_ref[...] * xt_ref[...]
        ue_ref[0:CONV_HALO, :] = gch_ref[...] * xth_ref[...] * keep
        ue_ref[CONV_HALO:CONV_HALO + ROW_BLK, :] = u
        ut_ref[...] = u[ROW_BLK - CONV_HALO:ROW_BLK, :]
        pv = pv_ref[...]
        ve_ref[0:POOL_HALO, :] = pvh_ref[...] * keep
        ve_ref[POOL_HALO:POOL_HALO + ROW_BLK, :] = pv
        pos1 = (n * ROW_BLK + 1 + lax.broadcasted_iota(jnp.int32, (ROW_BLK, 1), 0)).astype(F32)

        def cnt_of(w):
            return jnp.minimum(pos1, float(w))

        _conv_pool(ue_ref, ve_ref, CONV_HALO, POOL_HALO, 1, ROW_BLK, gb_ref[...], pv, cw_ref, pw_ref, ps_ref,
                   cnt_of, o_ref)

    @pl.when(i >= n_blocks)
    def _():
        o_ref[...] = tail_ref[...]


def _convpool_prompt(rest, tail, conv_w, pool_w, pool_scale3, layer, n_prompt_rows, blocks_per_seq):
    c = rest.shape[1] // 4
    n_blocks = n_prompt_rows // ROW_BLK
    n_seq = n_blocks // blocks_per_seq
    ch, ph = ROW_BLK // CONV_HALO, ROW_BLK // POOL_HALO

    def blk(col):
        return pl.BlockSpec((ROW_BLK, c), lambda i: (i, col))

    def halo(rows, per_blk, col):
        return pl.BlockSpec((rows, c), lambda i: (jnp.maximum(i * per_blk - 1, 0), col))

    return pl.pallas_call(
        functools.partial(_convpool_prompt_kernel, blocks_per_seq=blocks_per_seq, n_blocks=n_blocks),
        grid=(n_blocks + 1,),
        in_specs=[
            blk(0), blk(1), blk(2), blk(3),
            halo(CONV_HALO, ch, 1), halo(CONV_HALO, ch, 2), halo(POOL_HALO, ph, 3),
            pl.BlockSpec((None, CONV_W, c), lambda i: (layer, 0, 0)),
            pl.BlockSpec((None, len(POOL_WINDOWS), c // 4, c // 4), lambda i: (layer, 0, 0, 0)),
            pl.BlockSpec((None, 1, c), lambda i: (layer, 0, 0)),
            pl.BlockSpec((ROW_BLK, 2 * c), lambda i: (0, 0)),
        ],
        out_specs=[
            pl.BlockSpec((ROW_BLK, 2 * c), lambda i: (i, 0)),
            pl.BlockSpec((CONV_HALO, c), lambda i: (jnp.minimum(i // blocks_per_seq, n_seq - 1), 0)),
        ],
        out_shape=[jax.ShapeDtypeStruct((n_prompt_rows + ROW_BLK, 2 * c), BF16),
                   jax.ShapeDtypeStruct((n_seq * CONV_HALO, c), F32)],
        scratch_shapes=[pltpu.VMEM((CONV_HALO + ROW_BLK, c), F32), pltpu.VMEM((POOL_HALO + ROW_BLK, c), F32)],
        compiler_params=_cparams(("arbitrary",)),
        name="convpool_prompt",
    )(rest, rest, rest, rest, rest, rest, rest, conv_w, pool_w, pool_scale3, tail)


def _convpool_sample_kernel(gb_ref, gc_ref, xt_ref, pv_ref, cprev_ref, pprev_ref, cw_ref, pw_ref, ps_ref,
                            o_ref, u_ref, ue_ref, ve_ref, *, stride, pos0):
    rows = gb_ref.shape[0]
    ch, ph = cprev_ref.shape[0], pprev_ref.shape[0]
    u = gc_ref[...] * xt_ref[...]
    u_ref[...] = u
    ue_ref[0:ch, :] = cprev_ref[...]
    ue_ref[ch:ch + rows, :] = u
    pv = pv_ref[...]
    ve_ref[0:ph, :] = pprev_ref[...]
    ve_ref[ph:ph + rows, :] = pv
    pos1 = (pos0 + 1 + lax.broadcasted_iota(jnp.int32, (rows, 1), 0) // stride).astype(F32)

    def cnt_of(w):
        return jnp.minimum(pos1, float(w))

    _conv_pool(ue_ref, ve_ref, ch, ph, stride, rows, gb_ref[...], pv, cw_ref, pw_ref, ps_ref, cnt_of, o_ref)


def _convpool_sample(rest, row_blk_idx, cprev, pprev, conv_w, pool_w, pool_scale3, layer, stride, pos0):
    c = rest.shape[1] // 4
    rows = ROW_BLK

    def blk(col):
        return pl.BlockSpec((rows, c), lambda i: (row_blk_idx, col))

    return pl.pallas_call(
        functools.partial(_convpool_sample_kernel, stride=stride, pos0=pos0),
        grid=(1,),
        in_specs=[
            blk(0), blk(1), blk(2), blk(3),
            pl.BlockSpec(cprev.shape, lambda i: (0, 0)),
            pl.BlockSpec(pprev.shape, lambda i: (0, 0)),
            pl.BlockSpec((None, CONV_W, c), lambda i: (layer, 0, 0)),
            pl.BlockSpec((None, len(POOL_WINDOWS), c // 4, c // 4), lambda i: (layer, 0, 0, 0)),
            pl.BlockSpec((None, 1, c), lambda i: (layer, 0, 0)),
        ],
        out_specs=[pl.BlockSpec((rows, 2 * c), lambda i: (0, 0)), pl.BlockSpec((rows, c), lambda i: (0, 0))],
        out_shape=[jax.ShapeDtypeStruct((rows, 2 * c), BF16), jax.ShapeDtypeStruct((rows, c), F32)],
        scratch_shapes=[pltpu.VMEM((cprev.shape[0] + rows, c), F32), pltpu.VMEM((pprev.shape[0] + rows, c), F32)],
        compiler_params=_cparams(("arbitrary",)),
        name="convpool_sample",
    )(rest, rest, rest, rest, cprev, pprev, conv_w, pool_w, pool_scale3)


TM = 2080
TM_MOE = TM + SUB_ROWS
TN_PROJ = 512
TN_FF = 256
TN_DOWN = 1024
TK_DOWN = 1024
DISPATCH_ROWS = SUB_ROWS
KN_PAD = 16


def kernel(x_prompt, x_sample, cache_k, cache_v, state_conv, state_pool, c_prompt, c_sample, rel_bias, w_ada, b_ada, norm_g, w_in, sinks, conv_w, pool_w, pool_scale, w_out, w_gate_dense, w_up_dense, w_down_dense, w_router, w_gate_moe, w_up_moe, w_down_moe):
    bp, t, d = x_prompt.shape
    bs, s, _ = x_sample.shape
    depth = w_ada.shape[0]
    wb = cache_k.shape[2]
    n_p = bp * t
    n_s = bs * s
    m = n_p + n_s
    assert n_s == ROW_BLK and t % ROW_BLK == 0 and m % TM == 0
    blocks_per_seq = t // ROW_BLK
    d_qkv = D_ATTN + 2 * D_KV
    c_ch = (w_in.shape[2] - d_qkv) // 4

    x = (x_prompt.reshape(n_p, d), x_sample.transpose(1, 0, 2).reshape(n_s, d))

    n_c = bp + bs
    c_rows = -(-n_c // 16) * 16
    c_all = jnp.concatenate([c_prompt, c_sample, jnp.zeros((c_rows - n_c, d), F32)], axis=0)
    mod = _ada_mod(c_all, w_ada, b_ada)

    def mod_rows(l):
        ml = mod[l].reshape(c_rows, N_MOD, d).transpose(1, 0, 2)
        mp = jnp.broadcast_to(ml[:, :bp, None, :], (N_MOD, bp, ROW_BLK, d)).reshape(N_MOD, bp * ROW_BLK, d)
        ms = jnp.tile(ml[:, bp:bp + bs], (1, s, 1))
        return jnp.concatenate([mp, ms], axis=1)

    mods = [mod_rows(l) for l in range(depth)]
    norm_g4 = norm_g.reshape(depth, 4, 1, d)

    qi = jnp.arange(ROW_BLK)[:, None]
    dist_p = ROW_BLK + qi - jnp.arange(2 * ROW_BLK)[None, :]
    bias_p = _bias_table(rel_bias, _masked_bucket(dist_p))
    kn_pad = KN_PAD
    si = jnp.arange(8)[:, None]
    dist_s = si + wb - jnp.arange(wb + kn_pad)[None, :]
    bucket_s = jnp.where((si < s) & (jnp.arange(wb + kn_pad)[None, :] < wb + s), _masked_bucket(dist_s), -1)
    bias_s = _bias_table(rel_bias, bucket_s)[:, :s]
    bias_s = bias_s.reshape(N_KV_HEADS, GQA_GROUP, s, wb + kn_pad).transpose(0, 2, 1, 3)
    bias_s = bias_s.reshape(N_KV_HEADS, s * GQA_GROUP, wb + kn_pad)

    pool_scale3 = pool_scale.reshape(depth, 1, c_ch)
    w_router_pad = jnp.pad(w_router, ((0, 0), (0, 0), (0, LANES - N_EXPERTS)))

    h = _norm_mod(x[0], x[1], norm_g4, 0, mods[0], blocks_per_seq)
    kp_l, vp_l, cp_l, pp_l, ks_l, vs_l, cs_l, ps_l = ([] for _ in range(8))
    for l in range(depth):
        j = l // 2
        is_moe = l % 2 == 1
        qkv = _matmul(h, w_in, l, 0, d_qkv, TM, TN_PROJ)
        rest = _matmul(h, w_in, l, d_qkv // TN_PROJ, 4 * c_ch, TM, TN_PROJ)

        qs = qkv[n_p:, :D_ATTN].reshape(s, bs, N_KV_HEADS, GQA_GROUP, HEAD_DIM)
        qs = qs.transpose(1, 2, 0, 3, 4).reshape(bs, N_KV_HEADS, s * GQA_GROUP, HEAD_DIM)
        k_new = qkv[n_p:, D_ATTN:D_ATTN + D_KV].reshape(s, bs, N_KV_HEADS, HEAD_DIM).transpose(1, 0, 2, 3)
        v_new = qkv[n_p:, D_ATTN + D_KV:].reshape(s, bs, N_KV_HEADS, HEAD_DIM).transpose(1, 0, 2, 3)
        pad_new = lambda a: jnp.pad(a.transpose(0, 2, 1, 3), ((0, 0), (0, 0), (0, kn_pad - s), (0, 0)))
        sink_s = jnp.tile(sinks[l].reshape(N_KV_HEADS, 1, GQA_GROUP), (1, s, 1)).reshape(N_KV_HEADS, s * GQA_GROUP, 1)
        o_s = _attn_sample(qs, cache_k[l].transpose(0, 2, 1, 3), cache_v[l].transpose(0, 2, 1, 3),
                           pad_new(k_new), pad_new(v_new), bias_s, sink_s)
        o_s = o_s.reshape(bs, N_KV_HEADS, s, GQA_GROUP, HEAD_DIM).transpose(2, 0, 1, 3, 4).reshape(n_s, D_ATTN)
        a_all = _attn_prompt(qkv, sinks, bias_p, o_s.astype(BF16), l, n_p, blocks_per_seq)

        cprev = state_conv[l].transpose(1, 0, 2).reshape((CONV_W - 1) * bs, c_ch)
        pprev = state_pool[l].transpose(1, 0, 2).reshape(POOL_PAD * bs, c_ch)
        cp_s, u_s = _convpool_sample(rest, n_p // ROW_BLK, cprev, pprev, conv_w, pool_w, pool_scale3, l, bs, PAST_LEN)
        cp_all, u_tail = _convpool_prompt(rest, cp_s, conv_w, pool_w, pool_scale3, l, n_p, blocks_per_seq)

        def seq_tails(a, n_rows, c0, c1):
            return jnp.stack([a[(b + 1) * t - n_rows:(b + 1) * t, c0:c1] for b in range(bp)])

        kp_l.append(seq_tails(qkv, WINDOW, D_ATTN, D_ATTN + D_KV).reshape(bp, WINDOW, N_KV_HEADS, HEAD_DIM))
        vp_l.append(seq_tails(qkv, WINDOW, D_ATTN + D_KV, d_qkv).reshape(bp, WINDOW, N_KV_HEADS, HEAD_DIM))
        cp_l.append(u_tail.reshape(bp, CONV_HALO, c_ch)[:, CONV_HALO - (CONV_W - 1):])
        pp_l.append(seq_tails(rest, POOL_PAD, 3 * c_ch, 4 * c_ch))
        ks_l.append(jnp.concatenate([cache_k[l], k_new], axis=1)[:, s:])
        vs_l.append(jnp.concatenate([cache_v[l], v_new], axis=1)[:, s:])
        u_bs = u_s.reshape(s, bs, c_ch).transpose(1, 0, 2)
        cs_l.append(jnp.concatenate([state_conv[l], u_bs], axis=1)[:, s:])
        pv_bs = rest[n_p:, 3 * c_ch:].reshape(s, bs, c_ch).transpose(1, 0, 2)
        ps_l.append(jnp.concatenate([state_pool[l], pv_bs], axis=1)[:, s:])

        mo = _matmul2(a_all, cp_all, w_out, l, TM, TN_PROJ)
        nxt2 = (l, 2, mods[l], 4, 3)
        last = l == depth - 1
        nxt_layer = None if last else (l + 1, 0, mods[l + 1], 1, 0)
        split_out = n_p // ROW_BLK if last else 0
        if not is_moe:
            x, h2 = _post(x, mo, norm_g4, l, 1, mods[l], 2, blocks_per_seq, nxt=nxt2)
            act = _gateup(h2, w_gate_dense, w_up_dense, j, TM, TN_FF)
            y = _down(act, w_down_dense, j, TM, TN_DOWN, TK_DOWN)
            outs = _post(x, y, norm_g4, l, 3, mods[l], 5, blocks_per_seq, nxt=nxt_layer, split_out=split_out)
        else:
            x, h2, h2f, ridx, rw = _post(x, mo, norm_g4, l, 1, mods[l], 2, blocks_per_seq, nxt=nxt2,
                                          w_router_pad=w_router_pad[j])
            n_tiles = (m * TOP_K) // TM_MOE + N_EXPERTS
            pos, src, tile_e, tile_rows, n_used = _route_meta(ridx, TM_MOE, n_tiles)
            hs = _dispatch(h2f, src, tile_rows, TM_MOE, DISPATCH_ROWS)
            act = _gateup_grouped(hs, w_gate_moe, w_up_moe, j, tile_e, tile_rows, n_used, TM_MOE, TN_FF)
            ys = _down_grouped(act, w_down_moe, j, tile_e, tile_rows, n_used, TM_MOE, TN_DOWN, TK_DOWN)
            pos_blk = pos.reshape(m // ROW_BLK, ROW_BLK, TOP_K).transpose(0, 2, 1).reshape(m // ROW_BLK, 1, TOP_K * ROW_BLK)
            outs = _post(x, ys, norm_g4, l, 3, mods[l], 5, blocks_per_seq, nxt=nxt_layer, gather=(pos_blk, rw),
                         split_out=split_out)
        if last:
            x = tuple(outs)
        else:
            x, h = outs

    y_prompt = x[0].reshape(bp, t, d)
    y_sample = x[1].reshape(s, bs, d).transpose(1, 0, 2)
    st = jnp.stack
    return (y_prompt, y_sample, st(kp_l), st(vp_l), st(cp_l), st(pp_l), st(ks_l), st(vs_l), st(cs_l), st(ps_l))
```

```python
import functools
import math

import jax
import jax.numpy as jnp
from jax import lax
from jax.experimental import pallas as pl
from jax.experimental.pallas import tpu as pltpu

F32 = jnp.float32
BF16 = jnp.bfloat16

N_HEADS = 32
N_KV_HEADS = 4
HEAD_DIM = 64
GQA_GROUP = N_HEADS // N_KV_HEADS
D_ATTN = N_HEADS * HEAD_DIM
D_KV = N_KV_HEADS * HEAD_DIM
WINDOW = 128
N_BUCKETS = 32
MAX_EXACT = N_BUCKETS // 2
MAX_DISTANCE = WINDOW
CONV_W = 3
POOL_WINDOWS = (2, 4, 8, 16)
POOL_PAD = max(POOL_WINDOWS) - 1
N_EXPERTS = 8
TOP_K = 2
N_MOD = 6
RMS_EPS = 1e-6
PAST_LEN = 16384

ROW_BLK = 128
LANES = 128
VMEM_LIMIT = 56 * 1024 * 1024


def _cparams(sem):
    return pltpu.CompilerParams(dimension_semantics=sem, vmem_limit_bytes=VMEM_LIMIT)


def _rms(x, g):
    return x * lax.rsqrt(jnp.mean(x * x, axis=-1, keepdims=True) + RMS_EPS) * g


def _silu(x):
    return x / (1.0 + jnp.exp(-x))


def _ada_kernel(c_ref, w_ref, b_ref, o_ref):
    s = _silu(c_ref[...]).astype(BF16)
    o_ref[...] = jnp.dot(s, w_ref[...].astype(BF16), preferred_element_type=F32) + b_ref[...]


def _ada_mod(c_pad, w_ada, b_ada, tn=512):
    depth, d, n = w_ada.shape
    rows = c_pad.shape[0]
    return pl.pallas_call(
        _ada_kernel,
        grid=(depth, n // tn),
        in_specs=[
            pl.BlockSpec((rows, d), lambda l, j: (0, 0)),
            pl.BlockSpec((None, d, tn), lambda l, j: (l, 0, j)),
            pl.BlockSpec((None, 1, tn), lambda l, j: (l, 0, j)),
        ],
        out_specs=pl.BlockSpec((None, rows, tn), lambda l, j: (l, 0, j)),
        out_shape=jax.ShapeDtypeStruct((depth, rows, n), F32),
        compiler_params=_cparams(("arbitrary", "arbitrary")),
        name="ada_mod",
    )(c_pad, w_ada, b_ada.reshape(depth, 1, n))


def _bias_kernel(rb_ref, bucket_ref, o_ref):
    h = pl.program_id(0)
    bucket = bucket_ref[...]
    acc = jnp.full(bucket.shape, -jnp.inf, F32)
    for b in range(N_BUCKETS):
        acc = jnp.where(bucket == b, rb_ref[b, h], acc)
    o_ref[...] = acc


def _bias_table(rel_bias, bucket):
    rq, rk = bucket.shape
    return pl.pallas_call(
        _bias_kernel,
        grid=(N_HEADS,),
        in_specs=[
            pl.BlockSpec(memory_space=pltpu.SMEM),
            pl.BlockSpec((rq, rk), lambda h: (0, 0)),
        ],
        out_specs=pl.BlockSpec((None, rq, rk), lambda h: (h, 0, 0)),
        out_shape=jax.ShapeDtypeStruct((N_HEADS, rq, rk), F32),
        compiler_params=_cparams(("arbitrary",)),
        name="bias_table",
    )(rel_bias, bucket)


def _t5_bucket(dist):
    n = jnp.maximum(dist, 0)
    nf = jnp.maximum(n, 1).astype(F32)
    large = MAX_EXACT + (jnp.log(nf / MAX_EXACT) / math.log(MAX_DISTANCE / MAX_EXACT)
                         * (N_BUCKETS - MAX_EXACT)).astype(jnp.int32)
    large = jnp.minimum(large, N_BUCKETS - 1)
    return jnp.where(n < MAX_EXACT, n, large)


def _masked_bucket(dist):
    return jnp.where((dist >= 0) & (dist < WINDOW), _t5_bucket(dist), -1).astype(jnp.int32)


def _mod_spec(which, blocks_per_group, n_groups, d):
    last = n_groups - 1
    return pl.BlockSpec((None, ROW_BLK, d),
                        lambda i: (which, jnp.minimum(i // blocks_per_group, last), 0))


def _split_specs(n_lead_blocks, d):
    lead = pl.BlockSpec((ROW_BLK, d), lambda i: (jnp.minimum(i, n_lead_blocks - 1), 0))
    tail = pl.BlockSpec((ROW_BLK, d), lambda i: (0, 0))
    return [lead, tail]


def _read_split(lead_ref, tail_ref, n_lead_blocks):
    return jnp.where(pl.program_id(0) < n_lead_blocks, lead_ref[...], tail_ref[...])


def _norm_mod_kernel(xp_ref, xs_ref, g_ref, sc_ref, sh_ref, h_ref, *, n_lead_blocks):
    x = _read_split(xp_ref, xs_ref, n_lead_blocks)
    h = _rms(x, g_ref[...]) * (1.0 + sc_ref[...]) + sh_ref[...]
    h_ref[...] = h.astype(BF16)


def _norm_mod(x_lead, x_tail, norm_g4, layer, mods, blocks_per_group):
    n_lead, d = x_lead.shape
    m = n_lead + x_tail.shape[0]
    n_groups = mods.shape[1] // ROW_BLK
    return pl.pallas_call(
        functools.partial(_norm_mod_kernel, n_lead_blocks=n_lead // ROW_BLK),
        grid=(m // ROW_BLK,),
        in_specs=_split_specs(n_lead // ROW_BLK, d) + [
            pl.BlockSpec((None, None, 1, d), lambda i: (layer, 0, 0, 0)),
            _mod_spec(1, blocks_per_group, n_groups, d),
            _mod_spec(0, blocks_per_group, n_groups, d),
        ],
        out_specs=pl.BlockSpec((ROW_BLK, d), lambda i: (i, 0)),
        out_shape=jax.ShapeDtypeStruct((m, d), BF16),
        compiler_params=_cparams(("arbitrary",)),
        name="norm_mod",
    )(x_lead, x_tail, norm_g4, mods, mods)


def _top2(logits):
    lane = lax.broadcasted_iota(jnp.int32, logits.shape, 1).astype(F32)
    lg = jnp.where(lane < N_EXPERTS, logits, -jnp.inf)
    m1 = jnp.max(lg, axis=1, keepdims=True)
    i1 = jnp.min(jnp.where(lg == m1, lane, float(LANES)), axis=1, keepdims=True)
    lg2 = jnp.where(lane == i1, -jnp.inf, lg)
    m2 = jnp.max(lg2, axis=1, keepdims=True)
    i2 = jnp.min(jnp.where(lg2 == m2, lane, float(LANES)), axis=1, keepdims=True)
    e2 = jnp.exp(m2 - m1)
    w1 = 1.0 / (1.0 + e2)
    w2 = e2 / (1.0 + e2)
    ridx = jnp.where(lane == 0, i1, jnp.where(lane == 1, i2, 0.0)).astype(jnp.int32)
    rw = jnp.where(lane == 0, w1, jnp.where(lane == 1, w2, 0.0))
    return ridx, rw


def _post_kernel(*refs, has_next, route, gather, split_in, split_out):
    it = iter(refs)
    if gather:
        pos_ref, pos_next_ref, rw_ref, ys_hbm = next(it), next(it), next(it), next(it)
    else:
        y_ref = next(it)
    x_ref = next(it)
    if split_in:
        xs_ref = next(it)
    ga_ref, gate_ref = next(it), next(it)
    if has_next:
        gb_ref, sc_ref, sh_ref = next(it), next(it), next(it)
    if route:
        wr_ref = next(it)
    xo_ref = next(it)
    if split_out:
        xos_ref = next(it)
    if has_next:
        h_ref = next(it)
    if route:
        hf_ref, ridx_ref, rwo_ref = next(it), next(it), next(it)
    if gather:
        buf, sem = next(it), next(it)

    if gather:
        n_rows = TOP_K * ROW_BLK
        step = pl.program_id(0)
        n_steps = pl.num_programs(0)

        @pl.when(step == 0)
        def _():
            _gather_rows(ys_hbm, pos_ref, buf, sem, 0, n_rows, start=True)

        @pl.when(step + 1 < n_steps)
        def _():
            _gather_rows(ys_hbm, pos_next_ref, buf, sem, (step + 1) % 2, n_rows, start=True)

        _gather_rows(ys_hbm, pos_ref, buf, sem, step % 2, n_rows, start=False)
        rw = rw_ref[...]
        cur = buf.at[step % 2]
        y = rw[:, 0:1] * cur[0:ROW_BLK, :] + rw[:, 1:2] * cur[ROW_BLK:n_rows, :]
    else:
        y = y_ref[...]

    x_in = _read_split(x_ref, xs_ref, split_in) if split_in else x_ref[...]
    x = x_in + gate_ref[...] * _rms(y, ga_ref[...])
    if split_out:
        @pl.when(pl.program_id(0) < split_out)
        def _():
            xo_ref[...] = x

        @pl.when(pl.program_id(0) >= split_out)
        def _():
            xos_ref[...] = x
    else:
        xo_ref[...] = x
    if has_next:
        h = _rms(x, gb_ref[...]) * (1.0 + sc_ref[...]) + sh_ref[...]
        h_ref[...] = h.astype(BF16)
        if route:
            hf_ref[...] = h
            logits = jnp.dot(h.astype(BF16), wr_ref[...].astype(BF16), preferred_element_type=F32)
            ridx, rw_out = _top2(logits)
            ridx_ref[...] = ridx
            rwo_ref[...] = rw_out


def _post(x, y, norm_g4, layer, ga_idx, mods, gate_idx, blocks_per_group, *,
          nxt=None, w_router_pad=None, gather=None, split_out=0):
    split_in = 0
    if isinstance(x, tuple):
        x_lead, x_tail = x
        split_in = x_lead.shape[0] // ROW_BLK
        m, d = x_lead.shape[0] + x_tail.shape[0], x_lead.shape[1]
    else:
        m, d = x.shape
    n_groups = mods.shape[1] // ROW_BLK
    row = pl.BlockSpec((ROW_BLK, d), lambda i: (i, 0))
    lane_row = pl.BlockSpec((ROW_BLK, LANES), lambda i: (i, 0))
    has_next = nxt is not None
    route = w_router_pad is not None
    args, specs, scratch = [], [], []
    if gather is not None:
        pos, rw = gather
        n_blk = m // ROW_BLK

        def pos_spec(ahead):
            return pl.BlockSpec((None, 1, TOP_K * ROW_BLK), lambda i: (jnp.minimum(i + ahead, n_blk - 1), 0, 0),
                                memory_space=pltpu.SMEM)

        args += [pos, pos, rw, y]
        specs += [pos_spec(0), pos_spec(1), lane_row, pl.BlockSpec(memory_space=pl.ANY)]
        scratch = [pltpu.VMEM((2, TOP_K * ROW_BLK, d), F32), pltpu.SemaphoreType.DMA((2,))]
    else:
        args.append(y)
        specs.append(row)
    if split_in:
        args += [x_lead, x_tail]
        specs += _split_specs(split_in, d)
    else:
        args.append(x)
        specs.append(row)
    args += [norm_g4, mods]
    specs += [pl.BlockSpec((None, None, 1, d), lambda i: (layer, ga_idx, 0, 0)),
              _mod_spec(gate_idx, blocks_per_group, n_groups, d)]
    if split_out:
        out_shape = [jax.ShapeDtypeStruct((split_out * ROW_BLK, d), F32),
                     jax.ShapeDtypeStruct((m - split_out * ROW_BLK, d), F32)]
        out_specs = _split_specs(split_out, d)
    else:
        out_shape = [jax.ShapeDtypeStruct((m, d), F32)]
        out_specs = [row]
    if has_next:
        n_layer, n_gidx, n_mods, sc_idx, sh_idx = nxt
        args += [norm_g4, n_mods, n_mods]
        specs += [pl.BlockSpec((None, None, 1, d), lambda i: (n_layer, n_gidx, 0, 0)),
                  _mod_spec(sc_idx, blocks_per_group, n_groups, d),
                  _mod_spec(sh_idx, blocks_per_group, n_groups, d)]
        out_shape.append(jax.ShapeDtypeStruct((m, d), BF16))
        out_specs.append(row)
    if route:
        args.append(w_router_pad)
        specs.append(pl.BlockSpec((d, LANES), lambda i: (0, 0)))
        out_shape += [jax.ShapeDtypeStruct((m, d), F32), jax.ShapeDtypeStruct((m, LANES), jnp.int32),
                      jax.ShapeDtypeStruct((m, LANES), F32)]
        out_specs += [row, lane_row, lane_row]
    return pl.pallas_call(
        functools.partial(_post_kernel, has_next=has_next, route=route, gather=gather is not None,
                          split_in=split_in, split_out=split_out),
        grid=(m // ROW_BLK,),
        in_specs=specs,
        out_specs=out_specs,
        out_shape=out_shape,
        scratch_shapes=scratch,
        compiler_params=_cparams(("arbitrary",)),
        name="post_gather" if gather is not None else ("post_route" if route else "post"),
    )(*args)


def _row_resident(shape, index_map):
    return pl.BlockSpec(shape, index_map, pipeline_mode=pl.Buffered(1))


def _mm_kernel(x_ref, w_ref, o_ref):
    o_ref[...] = jnp.dot(x_ref[...], w_ref[...].astype(BF16), preferred_element_type=F32).astype(o_ref.dtype)


def _matmul(x, w_stack, layer, col_blk0, n_out, tm, tn):
    m, k = x.shape
    return pl.pallas_call(
        _mm_kernel,
        grid=(m // tm, n_out // tn),
        in_specs=[
            _row_resident((tm, k), lambda i, j: (i, 0)),
            pl.BlockSpec((None, k, tn), lambda i, j: (layer, 0, col_blk0 + j)),
        ],
        out_specs=pl.BlockSpec((tm, tn), lambda i, j: (i, j)),
        out_shape=jax.ShapeDtypeStruct((m, n_out), F32),
        compiler_params=_cparams(("arbitrary", "arbitrary")),
        name="matmul",
    )(x, w_stack)


def _mm2_kernel(a_ref, c_ref, w_ref, o_ref):
    ka = a_ref.shape[1]
    acc = jnp.dot(a_ref[...], w_ref[0:ka, :].astype(BF16), preferred_element_type=F32)
    acc = acc + jnp.dot(c_ref[...], w_ref[ka:, :].astype(BF16), preferred_element_type=F32)
    o_ref[...] = acc


def _matmul2(a, c, w_stack, layer, tm, tn):
    m, ka = a.shape
    kc = c.shape[1]
    n = w_stack.shape[2]
    return pl.pallas_call(
        _mm2_kernel,
        grid=(m // tm, n // tn),
        in_specs=[
            _row_resident((tm, ka), lambda i, j: (i, 0)),
            _row_resident((tm, kc), lambda i, j: (i, 0)),
            pl.BlockSpec((None, ka + kc, tn), lambda i, j: (layer, 0, j)),
        ],
        out_specs=pl.BlockSpec((tm, tn), lambda i, j: (i, j)),
        out_shape=jax.ShapeDtypeStruct((m, n), F32),
        compiler_params=_cparams(("arbitrary", "arbitrary")),
        name="matmul_out",
    )(a, c, w_stack)


SUB_ROWS = 208


def _swiglu(x, wg, wu):
    g = jnp.dot(x, wg, preferred_element_type=F32)
    u = jnp.dot(x, wu, preferred_element_type=F32)
    return (_silu(g) * u).astype(BF16)


def _gateup_body(x_ref, wg_ref, wu_ref, o_ref):
    o_ref[...] = _swiglu(x_ref[...], wg_ref[...].astype(BF16), wu_ref[...].astype(BF16))


MED_ROWS = 1040


def _tile_plan(rows, tm):
    n_big_sub = tm // SUB_ROWS - 1
    n_med_sub = MED_ROWS // SUB_ROWS
    n_valid = (rows + SUB_ROWS - 1) // SUB_ROWS
    use_big = n_valid >= n_big_sub
    use_med = jnp.logical_and(jnp.logical_not(use_big), n_valid >= n_med_sub)
    first = jnp.where(use_big, n_big_sub, jnp.where(use_med, n_med_sub, 0))
    return use_big, use_med, first, n_valid


def _for_sub_blocks(lo, hi, fn):
    def body(r, c):
        fn(pl.multiple_of(r * SUB_ROWS, SUB_ROWS))
        return c

    lax.fori_loop(lo, hi, body, 0)


def _gateup_kernel(x_ref, wg_ref, wu_ref, o_ref):
    _gateup_body(x_ref, wg_ref, wu_ref, o_ref)


def _gateup(x, wg_stack, wu_stack, layer, tm, tn):
    m, k = x.shape
    f = wg_stack.shape[2]
    w_spec = pl.BlockSpec((None, k, tn), lambda i, j: (layer, 0, j))
    return pl.pallas_call(
        _gateup_kernel,
        grid=(m // tm, f // tn),
        in_specs=[_row_resident((tm, k), lambda i, j: (i, 0)), w_spec, w_spec],
        out_specs=pl.BlockSpec((tm, tn), lambda i, j: (i, j)),
        out_shape=jax.ShapeDtypeStruct((m, f), BF16),
        compiler_params=_cparams(("arbitrary", "arbitrary")),
        name="gateup",
    )(x, wg_stack, wu_stack)


ACC_ROWS = 1040
ACC_COLS = 256


def _down_accumulate(a_ref, w, o_ref, n_rows):
    tn = o_ref.shape[1]
    for c0 in range(0, tn, ACC_COLS):
        wc = w[:, c0:c0 + ACC_COLS]
        for r0 in range(0, n_rows, ACC_ROWS):
            r1 = min(r0 + ACC_ROWS, n_rows)
            o_ref[r0:r1, c0:c0 + ACC_COLS] += jnp.dot(a_ref[r0:r1, :], wc, preferred_element_type=F32)


def _down_kernel(a_ref, w_ref, o_ref):
    @pl.when(pl.program_id(2) == 0)
    def _():
        o_ref[...] = jnp.zeros_like(o_ref)

    _down_accumulate(a_ref, w_ref[...].astype(BF16), o_ref, o_ref.shape[0])


def _down(a, wd_stack, layer, tm, tn, tk):
    m, f = a.shape
    d = wd_stack.shape[2]
    return pl.pallas_call(
        _down_kernel,
        grid=(m // tm, d // tn, f // tk),
        in_specs=[
            pl.BlockSpec((tm, tk), lambda i, j, k: (i, k)),
            pl.BlockSpec((None, tk, tn), lambda i, j, k: (layer, k, j)),
        ],
        out_specs=pl.BlockSpec((tm, tn), lambda i, j, k: (i, j)),
        out_shape=jax.ShapeDtypeStruct((m, d), F32),
        compiler_params=_cparams(("arbitrary", "arbitrary", "arbitrary")),
        name="down",
    )(a, wd_stack)


def _gateup_grouped_kernel(te_ref, tr_ref, nu_ref, x_ref, wg_ref, wu_ref, o_ref):
    tm = x_ref.shape[0]
    big_rows = tm - SUB_ROWS
    use_big, use_med, first, n_valid = _tile_plan(tr_ref[pl.program_id(0)], tm)

    @pl.when(n_valid > 0)
    def _():
        wg = wg_ref[...].astype(BF16)
        wu = wu_ref[...].astype(BF16)

        @pl.when(use_big)
        def _():
            o_ref[0:big_rows, :] = _swiglu(x_ref[0:big_rows, :], wg, wu)

        @pl.when(use_med)
        def _():
            o_ref[0:MED_ROWS, :] = _swiglu(x_ref[0:MED_ROWS, :], wg, wu)

        def compute(rs):
            o_ref[pl.ds(rs, SUB_ROWS), :] = _swiglu(x_ref[pl.ds(rs, SUB_ROWS), :], wg, wu)

        _for_sub_blocks(first, n_valid, compute)

    def clear(rs):
        o_ref[pl.ds(rs, SUB_ROWS), :] = jnp.zeros((SUB_ROWS, o_ref.shape[1]), o_ref.dtype)

    _for_sub_blocks(jnp.maximum(n_valid, first), tm // SUB_ROWS, clear)


def _gateup_grouped(xs, wg, wu, layer, tile_e, tile_rows, n_used, tm, tn):
    r, k = xs.shape
    f = wg.shape[3]
    nj = f // tn

    def w_map(i, j, te, tr, nu):
        return (layer, te[i], 0, jnp.where(i < nu[0], j, nj - 1))

    w_spec = pl.BlockSpec((None, None, k, tn), w_map)
    return pl.pallas_call(
        _gateup_grouped_kernel,
        grid_spec=pltpu.PrefetchScalarGridSpec(
            num_scalar_prefetch=3,
            grid=(r // tm, nj),
            in_specs=[_row_resident((tm, k), lambda i, j, te, tr, nu: (jnp.minimum(i, nu[0] - 1), 0)),
                      w_spec, w_spec],
            out_specs=pl.BlockSpec((tm, tn), lambda i, j, te, tr, nu: (i, j)),
        ),
        out_shape=jax.ShapeDtypeStruct((r, f), BF16),
        compiler_params=_cparams(("arbitrary", "arbitrary")),
        name="gateup_grouped",
    )(tile_e, tile_rows, n_used, xs, wg, wu)


def _down_grouped_kernel(te_ref, tr_ref, nu_ref, a_ref, w_ref, o_ref):
    tm = a_ref.shape[0]
    use_big, use_med, first, n_valid = _tile_plan(tr_ref[pl.program_id(0)], tm)

    @pl.when(pl.program_id(2) == 0)
    def _():
        o_ref[...] = jnp.zeros_like(o_ref)

    @pl.when(n_valid > 0)
    def _():
        w = w_ref[...].astype(BF16)

        @pl.when(use_big)
        def _():
            _down_accumulate(a_ref, w, o_ref, tm - SUB_ROWS)

        @pl.when(use_med)
        def _():
            _down_accumulate(a_ref, w, o_ref, MED_ROWS)

        def compute(rs):
            o_ref[pl.ds(rs, SUB_ROWS), :] += jnp.dot(a_ref[pl.ds(rs, SUB_ROWS), :], w, preferred_element_type=F32)

        _for_sub_blocks(first, n_valid, compute)


def _down_grouped(a, wd, layer, tile_e, tile_rows, n_used, tm, tn, tk):
    r, f = a.shape
    d = wd.shape[3]
    nk = f // tk
    nj = d // tn

    def a_map(i, j, k, te, tr, nu):
        return (jnp.minimum(i, nu[0] - 1), jnp.where(i < nu[0], k, nk - 1))

    def w_map(i, j, k, te, tr, nu):
        used = i < nu[0]
        return (layer, te[i], jnp.where(used, k, nk - 1), jnp.where(used, j, nj - 1))

    return pl.pallas_call(
        _down_grouped_kernel,
        grid_spec=pltpu.PrefetchScalarGridSpec(
            num_scalar_prefetch=3,
            grid=(r // tm, nj, nk),
            in_specs=[pl.BlockSpec((tm, tk), a_map), pl.BlockSpec((None, None, tk, tn), w_map)],
            out_specs=pl.BlockSpec((tm, tn), lambda i, j, k, te, tr, nu: (i, j)),
        ),
        out_shape=jax.ShapeDtypeStruct((r, d), F32),
        compiler_params=_cparams(("arbitrary", "arbitrary", "arbitrary")),
        name="down_grouped",
    )(tile_e, tile_rows, n_used, a, wd)


def _gather_rows(src_hbm, idx_ref, buf, sem, slot, n_rows, start):
    def row_copy(r, src_row):
        return pltpu.make_async_copy(src_hbm.at[pl.ds(src_row, 1), :], buf.at[slot, pl.ds(r, 1), :], sem.at[slot])

    def body(r, c):
        if start:
            row_copy(r, idx_ref[0, r]).start()
        else:
            row_copy(r, 0).wait()
        return c

    lax.fori_loop(0, n_rows, body, 0)


def _dispatch_kernel(tr_ref, src_ref, src_next_ref, h_hbm, o_ref, buf, sem, *, steps_per_tile, n_steps):
    rows = o_ref.shape[0]
    step = pl.program_id(0)

    def used(s):
        return (s % steps_per_tile) * rows < tr_ref[s // steps_per_tile]

    @pl.when(jnp.logical_and(step == 0, used(0)))
    def _():
        _gather_rows(h_hbm, src_ref, buf, sem, 0, rows, start=True)

    nxt = jnp.minimum(step + 1, n_steps - 1)

    @pl.when(jnp.logical_and(step + 1 < n_steps, used(nxt)))
    def _():
        _gather_rows(h_hbm, src_next_ref, buf, sem, (step + 1) % 2, rows, start=True)

    @pl.when(used(step))
    def _():
        _gather_rows(h_hbm, src_ref, buf, sem, step % 2, rows, start=False)
        o_ref[...] = buf[step % 2].astype(BF16)

    @pl.when(jnp.logical_not(used(step)))
    def _():
        o_ref[...] = jnp.zeros_like(o_ref)


def _dispatch(h_f32, src, tile_rows, tm, rows_per_step):
    d = h_f32.shape[1]
    r = src.shape[0]
    steps = r // rows_per_step
    src3 = src.reshape(steps, 1, rows_per_step)

    def idx_spec(ahead):
        return pl.BlockSpec((None, 1, rows_per_step), lambda i, tr: (jnp.minimum(i + ahead, steps - 1), 0, 0),
                            memory_space=pltpu.SMEM)

    return pl.pallas_call(
        functools.partial(_dispatch_kernel, steps_per_tile=tm // rows_per_step, n_steps=steps),
        grid_spec=pltpu.PrefetchScalarGridSpec(
            num_scalar_prefetch=1,
            grid=(steps,),
            in_specs=[idx_spec(0), idx_spec(1), pl.BlockSpec(memory_space=pl.ANY)],
            out_specs=pl.BlockSpec((rows_per_step, d), lambda i, tr: (i, 0)),
            scratch_shapes=[pltpu.VMEM((2, rows_per_step, d), F32), pltpu.SemaphoreType.DMA((2,))],
        ),
        out_shape=jax.ShapeDtypeStruct((r, d), BF16),
        compiler_params=_cparams(("arbitrary",)),
        name="dispatch",
    )(tile_rows, src3, src3, h_f32)


def _route_meta(ridx, tm, n_tiles):
    m = ridx.shape[0]
    e_flat = ridx[:, :TOP_K].reshape(-1)
    onehot = (e_flat[:, None] == jnp.arange(N_EXPERTS, dtype=jnp.int32)[None, :]).astype(jnp.int32)
    csum = jnp.cumsum(onehot, axis=0)
    rank = jnp.take_along_axis(csum, e_flat[:, None], axis=1)[:, 0] - 1
    counts = csum[-1]
    ptiles = (counts + tm - 1) // tm
    tile_end = jnp.cumsum(ptiles)
    tile_start = tile_end - ptiles
    pos = (tile_start[e_flat] * tm + rank).astype(jnp.int32)
    n_used = tile_end[-1].astype(jnp.int32)
    tiles = jnp.arange(n_tiles, dtype=jnp.int32)
    tile_e_raw = jnp.sum((tiles[:, None] >= tile_end[None, :]).astype(jnp.int32), axis=1)
    last_e = tile_e_raw[jnp.maximum(n_used - 1, 0)]
    tile_e = jnp.minimum(jnp.where(tiles < n_used, tile_e_raw, last_e), N_EXPERTS - 1).astype(jnp.int32)
    tile_rows = jnp.clip(counts[tile_e] - (tiles - tile_start[tile_e]) * tm, 0, tm)
    tile_rows = jnp.where(tiles < n_used, tile_rows, 0).astype(jnp.int32)
    src = jnp.zeros((n_tiles * tm,), jnp.int32).at[pos].set(jnp.arange(m * TOP_K, dtype=jnp.int32) // TOP_K)
    return pos.reshape(m, TOP_K), src, tile_e, tile_rows, n_used.reshape(1)


def _softmax_sink_pv(s, sink, vv):
    m = jnp.maximum(jnp.max(s, axis=1, keepdims=True), sink)
    p = jnp.exp(s - m)
    den = jnp.sum(p, axis=1, keepdims=True) + jnp.exp(sink - m)
    return jnp.dot((p / den).astype(BF16), vv, preferred_element_type=F32)


def _qk(q, kk):
    return lax.dot_general(q, kk, (((1,), (1,)), ((), ())), preferred_element_type=F32) * (1.0 / math.sqrt(HEAD_DIM))


def _attn_prompt_kernel(sink_ref, q_ref, kp_ref, kc_ref, vp_ref, vc_ref, bias_ref, tail_ref, o_ref, *,
                        layer, blocks_per_seq, n_blocks):
    i = pl.program_id(0)

    @pl.when(i < n_blocks)
    def _():
        first = (i % blocks_per_seq) == 0
        col = lax.broadcasted_iota(jnp.int32, (ROW_BLK, 2 * ROW_BLK), 1)
        no_prev = jnp.logical_and(first, col < ROW_BLK)
        for kvh in range(N_KV_HEADS):
            cs = slice(kvh * HEAD_DIM, (kvh + 1) * HEAD_DIM)
            kk = jnp.concatenate([kp_ref[:, cs], kc_ref[:, cs]], axis=0).astype(BF16)
            vv = jnp.concatenate([vp_ref[:, cs], vc_ref[:, cs]], axis=0).astype(BF16)
            for g in range(GQA_GROUP):
                h = kvh * GQA_GROUP + g
                hs = slice(h * HEAD_DIM, (h + 1) * HEAD_DIM)
                s = _qk(q_ref[:, hs].astype(BF16), kk) + bias_ref[h]
                s = jnp.where(no_prev, -jnp.inf, s)
                o_ref[:, hs] = _softmax_sink_pv(s, sink_ref[layer, h], vv).astype(BF16)

    @pl.when(i >= n_blocks)
    def _():
        o_ref[...] = tail_ref[...]


def _attn_prompt(qkv, sinks, bias_p, tail, layer, n_prompt_rows, blocks_per_seq):
    kb = D_ATTN // D_KV
    n_blocks = n_prompt_rows // ROW_BLK

    def prev(i):
        return jnp.maximum(i - 1, 0)

    return pl.pallas_call(
        functools.partial(_attn_prompt_kernel, layer=layer, blocks_per_seq=blocks_per_seq, n_blocks=n_blocks),
        grid=(n_blocks + 1,),
        in_specs=[
            pl.BlockSpec(memory_space=pltpu.SMEM),
            pl.BlockSpec((ROW_BLK, D_ATTN), lambda i: (i, 0)),
            pl.BlockSpec((ROW_BLK, D_KV), lambda i: (prev(i), kb)),
            pl.BlockSpec((ROW_BLK, D_KV), lambda i: (i, kb)),
            pl.BlockSpec((ROW_BLK, D_KV), lambda i: (prev(i), kb + 1)),
            pl.BlockSpec((ROW_BLK, D_KV), lambda i: (i, kb + 1)),
            pl.BlockSpec((N_HEADS, ROW_BLK, 2 * ROW_BLK), lambda i: (0, 0, 0)),
            pl.BlockSpec((ROW_BLK, D_ATTN), lambda i: (0, 0)),
        ],
        out_specs=pl.BlockSpec((ROW_BLK, D_ATTN), lambda i: (i, 0)),
        out_shape=jax.ShapeDtypeStruct((n_prompt_rows + ROW_BLK, D_ATTN), BF16),
        compiler_params=_cparams(("arbitrary",)),
        name="attn_prompt",
    )(sinks, qkv, qkv, qkv, qkv, qkv, bias_p, tail)


def _attn_sample_kernel(q_ref, ck_ref, cv_ref, kn_ref, vn_ref, bias_ref, sink_ref, o_ref):
    for kvh in range(N_KV_HEADS):
        kk = jnp.concatenate([ck_ref[kvh], kn_ref[kvh]], axis=0).astype(BF16)
        vv = jnp.concatenate([cv_ref[kvh], vn_ref[kvh]], axis=0).astype(BF16)
        s = _qk(q_ref[kvh].astype(BF16), kk) + bias_ref[kvh]
        o_ref[kvh] = _softmax_sink_pv(s, sink_ref[kvh], vv)


def _attn_sample(q, ck, cv, kn, vn, bias_s, sink_s):
    b, _, rows, _ = q.shape
    wb = ck.shape[2]
    pad = kn.shape[2]

    def per_b(n):
        return pl.BlockSpec((None, N_KV_HEADS, n, HEAD_DIM), lambda i: (i, 0, 0, 0))

    return pl.pallas_call(
        _attn_sample_kernel,
        grid=(b,),
        in_specs=[
            per_b(rows), per_b(wb), per_b(wb), per_b(pad), per_b(pad),
            pl.BlockSpec((N_KV_HEADS, rows, wb + pad), lambda i: (0, 0, 0)),
            pl.BlockSpec((N_KV_HEADS, rows, 1), lambda i: (0, 0, 0)),
        ],
        out_specs=per_b(rows),
        out_shape=jax.ShapeDtypeStruct(q.shape, F32),
        compiler_params=_cparams(("arbitrary",)),
        name="attn_sample",
    )(q, ck, cv, kn, vn, bias_s, sink_s)


def _conv_pool(ue_ref, ve_ref, conv_base, pool_base, stride, rows, gate_b, pv, cw_ref, pw_ref, ps_ref, cnt_of, o_ref):
    d_conv = gate_b.shape[1]
    y = cw_ref[CONV_W - 1:CONV_W, :] * ue_ref[conv_base:conv_base + rows, :]
    for j in range(1, CONV_W):
        off = conv_base - j * stride
        y = y + cw_ref[CONV_W - 1 - j:CONV_W - j, :] * ue_ref[off:off + rows, :]
    o_ref[:, 0:d_conv] = (gate_b * y).astype(BF16)
    grp = pv.shape[1] // len(POOL_WINDOWS)
    for gi, w in enumerate(POOL_WINDOWS):
        cs = slice(gi * grp, (gi + 1) * grp)
        acc = ve_ref[pool_base:pool_base + rows, cs]
        for j in range(1, w):
            off = pool_base - j * stride
            acc = acc + ve_ref[off:off + rows, cs]
        g = acc / cnt_of(w) - pv[:, cs]
        yg = jnp.dot(g.astype(BF16), pw_ref[gi].astype(BF16), preferred_element_type=F32) * ps_ref[:, cs]
        o_ref[:, d_conv + gi * grp:d_conv + (gi + 1) * grp] = yg.astype(BF16)


CONV_HALO = 8
POOL_HALO = 16


def _convpool_prompt_kernel(gb_ref, gc_ref, xt_ref, pv_ref, gch_ref, xth_ref, pvh_ref, cw_ref, pw_ref, ps_ref,
                            tail_ref, o_ref, ut_ref, ue_ref, ve_ref, *, blocks_per_seq, n_blocks):
    i = pl.program_id(0)

    @pl.when(i < n_blocks)
    def _():
        n = i % blocks_per_seq
        keep = (n != 0).astype(F32)
        u = gc_ref[...] * xt_ref[...]
        ue_ref[0:CONV_HALO, :] = gch_ref[...] * xth_ref[...] * keep
        ue_ref[CONV_HALO:CONV_HALO + ROW_BLK, :] = u
        ut_ref[...] = u[ROW_BLK - CONV_HALO:ROW_BLK, :]
        pv = pv_ref[...]
        ve_ref[0:POOL_HALO, :] = pvh_ref[...] * keep
        ve_ref[POOL_HALO:POOL_HALO + ROW_BLK, :] = pv
        pos1 = (n * ROW_BLK + 1 + lax.broadcasted_iota(jnp.int32, (ROW_BLK, 1), 0)).astype(F32)

        def cnt_of(w):
            return jnp.minimum(pos1, float(w))

        _conv_pool(ue_ref, ve_ref, CONV_HALO, POOL_HALO, 1, ROW_BLK, gb_ref[...], pv, cw_ref, pw_ref, ps_ref,
                   cnt_of, o_ref)

    @pl.when(i >= n_blocks)
    def _():
        o_ref[...] = tail_ref[...]


def _convpool_prompt(rest, tail, conv_w, pool_w, pool_scale3, layer, n_prompt_rows, blocks_per_seq):
    c = rest.shape[1] // 4
    n_blocks = n_prompt_rows // ROW_BLK
    n_seq = n_blocks // blocks_per_seq
    ch, ph = ROW_BLK // CONV_HALO, ROW_BLK // POOL_HALO

    def blk(col):
        return pl.BlockSpec((ROW_BLK, c), lambda i: (i, col))

    def halo(rows, per_blk, col):
        return pl.BlockSpec((rows, c), lambda i: (jnp.maximum(i * per_blk - 1, 0), col))

    return pl.pallas_call(
        functools.partial(_convpool_prompt_kernel, blocks_per_seq=blocks_per_seq, n_blocks=n_blocks),
        grid=(n_blocks + 1,),
        in_specs=[
            blk(0), blk(1), blk(2), blk(3),
            halo(CONV_HALO, ch, 1), halo(CONV_HALO, ch, 2), halo(POOL_HALO, ph, 3),
            pl.BlockSpec((None, CONV_W, c), lambda i: (layer, 0, 0)),
            pl.BlockSpec((None, len(POOL_WINDOWS), c // 4, c // 4), lambda i: (layer, 0, 0, 0)),
            pl.BlockSpec((None, 1, c), lambda i: (layer, 0, 0)),
            pl.BlockSpec((ROW_BLK, 2 * c), lambda i: (0, 0)),
        ],
        out_specs=[
            pl.BlockSpec((ROW_BLK, 2 * c), lambda i: (i, 0)),
            pl.BlockSpec((CONV_HALO, c), lambda i: (jnp.minimum(i // blocks_per_seq, n_seq - 1), 0)),
        ],
        out_shape=[jax.ShapeDtypeStruct((n_prompt_rows + ROW_BLK, 2 * c), BF16),
                   jax.ShapeDtypeStruct((n_seq * CONV_HALO, c), F32)],
        scratch_shapes=[pltpu.VMEM((CONV_HALO + ROW_BLK, c), F32), pltpu.VMEM((POOL_HALO + ROW_BLK, c), F32)],
        compiler_params=_cparams(("arbitrary",)),
        name="convpool_prompt",
    )(rest, rest, rest, rest, rest, rest, rest, conv_w, pool_w, pool_scale3, tail)


def _convpool_sample_kernel(gb_ref, gc_ref, xt_ref, pv_ref, cprev_ref, pprev_ref, cw_ref, pw_ref, ps_ref,
                            o_ref, u_ref, ue_ref, ve_ref, *, stride, pos0):
    rows = gb_ref.shape[0]
    ch, ph = cprev_ref.shape[0], pprev_ref.shape[0]
    u = gc_ref[...] * xt_ref[...]
    u_ref[...] = u
    ue_ref[0:ch, :] = cprev_ref[...]
    ue_ref[ch:ch + rows, :] = u
    pv = pv_ref[...]
    ve_ref[0:ph, :] = pprev_ref[...]
    ve_ref[ph:ph + rows, :] = pv
    pos1 = (pos0 + 1 + lax.broadcasted_iota(jnp.int32, (rows, 1), 0) // stride).astype(F32)

    def cnt_of(w):
        return jnp.minimum(pos1, float(w))

    _conv_pool(ue_ref, ve_ref, ch, ph, stride, rows, gb_ref[...], pv, cw_ref, pw_ref, ps_ref, cnt_of, o_ref)


def _convpool_sample(rest, row_blk_idx, cprev, pprev, conv_w, pool_w, pool_scale3, layer, stride, pos0):
    c = rest.shape[1] // 4
    rows = ROW_BLK

    def blk(col):
        return pl.BlockSpec((rows, c), lambda i: (row_blk_idx, col))

    return pl.pallas_call(
        functools.partial(_convpool_sample_kernel, stride=stride, pos0=pos0),
        grid=(1,),
        in_specs=[
            blk(0), blk(1), blk(2), blk(3),
            pl.BlockSpec(cprev.shape, lambda i: (0, 0)),
            pl.BlockSpec(pprev.shape, lambda i: (0, 0)),
            pl.BlockSpec((None, CONV_W, c), lambda i: (layer, 0, 0)),
            pl.BlockSpec((None, len(POOL_WINDOWS), c // 4, c // 4), lambda i: (layer, 0, 0, 0)),
            pl.BlockSpec((None, 1, c), lambda i: (layer, 0, 0)),
        ],
        out_specs=[pl.BlockSpec((rows, 2 * c), lambda i: (0, 0)), pl.BlockSpec((rows, c), lambda i: (0, 0))],
        out_shape=[jax.ShapeDtypeStruct((rows, 2 * c), BF16), jax.ShapeDtypeStruct((rows, c), F32)],
        scratch_shapes=[pltpu.VMEM((cprev.shape[0] + rows, c), F32), pltpu.VMEM((pprev.shape[0] + rows, c), F32)],
        compiler_params=_cparams(("arbitrary",)),
        name="convpool_sample",
    )(rest, rest, rest, rest, cprev, pprev, conv_w, pool_w, pool_scale3)


TM = 2080
TM_MOE = TM + SUB_ROWS
TN_PROJ = 512
TN_FF = 256
TN_DOWN = 1024
TK_DOWN = 1024
DISPATCH_ROWS = SUB_ROWS
KN_PAD = 16


def kernel(x_prompt, x_sample, cache_k, cache_v, state_conv, state_pool, c_prompt, c_sample, rel_bias, w_ada, b_ada, norm_g, w_in, sinks, conv_w, pool_w, pool_scale, w_out, w_gate_dense, w_up_dense, w_down_dense, w_router, w_gate_moe, w_up_moe, w_down_moe):
    bp, t, d = x_prompt.shape
    bs, s, _ = x_sample.shape
    depth = w_ada.shape[0]
    wb = cache_k.shape[2]
    n_p = bp * t
    n_s = bs * s
    m = n_p + n_s
    assert n_s == ROW_BLK and t % ROW_BLK == 0 and m % TM == 0
    blocks_per_seq = t // ROW_BLK
    d_qkv = D_ATTN + 2 * D_KV
    c_ch = (w_in.shape[2] - d_qkv) // 4

    x = (x_prompt.reshape(n_p, d), x_sample.transpose(1, 0, 2).reshape(n_s, d))

    n_c = bp + bs
    c_rows = -(-n_c // 16) * 16
    c_all = jnp.concatenate([c_prompt, c_sample, jnp.zeros((c_rows - n_c, d), F32)], axis=0)
    mod = _ada_mod(c_all, w_ada, b_ada)

    def mod_rows(l):
        ml = mod[l].reshape(c_rows, N_MOD, d).transpose(1, 0, 2)
        mp = jnp.broadcast_to(ml[:, :bp, None, :], (N_MOD, bp, ROW_BLK, d)).reshape(N_MOD, bp * ROW_BLK, d)
        ms = jnp.tile(ml[:, bp:bp + bs], (1, s, 1))
        return jnp.concatenate([mp, ms], axis=1)

    mods = [mod_rows(l) for l in range(depth)]
    norm_g4 = norm_g.reshape(depth, 4, 1, d)

    qi = jnp.arange(ROW_BLK)[:, None]
    dist_p = ROW_BLK + qi - jnp.arange(2 * ROW_BLK)[None, :]
    bias_p = _bias_table(rel_bias, _masked_bucket(dist_p))
    kn_pad = KN_PAD
    si = jnp.arange(8)[:, None]
    dist_s = si + wb - jnp.arange(wb + kn_pad)[None, :]
    bucket_s = jnp.where((si < s) & (jnp.arange(wb + kn_pad)[None, :] < wb + s), _masked_bucket(dist_s), -1)
    bias_s = _bias_table(rel_bias, bucket_s)[:, :s]
    bias_s = bias_s.reshape(N_KV_HEADS, GQA_GROUP, s, wb + kn_pad).transpose(0, 2, 1, 3)
    bias_s = bias_s.reshape(N_KV_HEADS, s * GQA_GROUP, wb + kn_pad)

    pool_scale3 = pool_scale.reshape(depth, 1, c_ch)
    w_router_pad = jnp.pad(w_router, ((0, 0), (0, 0), (0, LANES - N_EXPERTS)))

    h = _norm_mod(x[0], x[1], norm_g4, 0, mods[0], blocks_per_seq)
    kp_l, vp_l, cp_l, pp_l, ks_l, vs_l, cs_l, ps_l = ([] for _ in range(8))
    for l in range(depth):
        j = l // 2
        is_moe = l % 2 == 1
        qkv = _matmul(h, w_in, l, 0, d_qkv, TM, TN_PROJ)
        rest = _matmul(h, w_in, l, d_qkv // TN_PROJ, 4 * c_ch, TM, TN_PROJ)

        qs = qkv[n_p:, :D_ATTN].reshape(s, bs, N_KV_HEADS, GQA_GROUP, HEAD_DIM)
        qs = qs.transpose(1, 2, 0, 3, 4).reshape(bs, N_KV_HEADS, s * GQA_GROUP, HEAD_DIM)
        k_new = qkv[n_p:, D_ATTN:D_ATTN + D_KV].reshape(s, bs, N_KV_HEADS, HEAD_DIM).transpose(1, 0, 2, 3)
        v_new = qkv[n_p:, D_ATTN + D_KV:].reshape(s, bs, N_KV_HEADS, HEAD_DIM).transpose(1, 0, 2, 3)
        pad_new = lambda a: jnp.pad(a.transpose(0, 2, 1, 3), ((0, 0), (0, 0), (0, kn_pad - s), (0, 0)))
        sink_s = jnp.tile(sinks[l].reshape(N_KV_HEADS, 1, GQA_GROUP), (1, s, 1)).reshape(N_KV_HEADS, s * GQA_GROUP, 1)
        o_s = _attn_sample(qs, cache_k[l].transpose(0, 2, 1, 3), cache_v[l].transpose(0, 2, 1, 3),
                           pad_new(k_new), pad_new(v_new), bias_s, sink_s)
        o_s = o_s.reshape(bs, N_KV_HEADS, s, GQA_GROUP, HEAD_DIM).transpose(2, 0, 1, 3, 4).reshape(n_s, D_ATTN)
        a_all = _attn_prompt(qkv, sinks, bias_p, o_s.astype(BF16), l, n_p, blocks_per_seq)

        cprev = state_conv[l].transpose(1, 0, 2).reshape((CONV_W - 1) * bs, c_ch)
        pprev = state_pool[l].transpose(1, 0, 2).reshape(POOL_PAD * bs, c_ch)
        cp_s, u_s = _convpool_sample(rest, n_p // ROW_BLK, cprev, pprev, conv_w, pool_w, pool_scale3, l, bs, PAST_LEN)
        cp_all, u_tail = _convpool_prompt(rest, cp_s, conv_w, pool_w, pool_scale3, l, n_p, blocks_per_seq)

        def seq_tails(a, n_rows, c0, c1):
            return jnp.stack([a[(b + 1) * t - n_rows:(b + 1) * t, c0:c1] for b in range(bp)])

        kp_l.append(seq_tails(qkv, WINDOW, D_ATTN, D_ATTN + D_KV).reshape(bp, WINDOW, N_KV_HEADS, HEAD_DIM))
        vp_l.append(seq_tails(qkv, WINDOW, D_ATTN + D_KV, d_qkv).reshape(bp, WINDOW, N_KV_HEADS, HEAD_DIM))
        cp_l.append(u_tail.reshape(bp, CONV_HALO, c_ch)[:, CONV_HALO - (CONV_W - 1):])
        pp_l.append(seq_tails(rest, POOL_PAD, 3 * c_ch, 4 * c_ch))
        ks_l.append(jnp.concatenate([cache_k[l], k_new], axis=1)[:, s:])
        vs_l.append(jnp.concatenate([cache_v[l], v_new], axis=1)[:, s:])
        u_bs = u_s.reshape(s, bs, c_ch).transpose(1, 0, 2)
        cs_l.append(jnp.concatenate([state_conv[l], u_bs], axis=1)[:, s:])
        pv_bs = rest[n_p:, 3 * c_ch:].reshape(s, bs, c_ch).transpose(1, 0, 2)
        ps_l.append(jnp.concatenate([state_pool[l], pv_bs], axis=1)[:, s:])

        mo = _matmul2(a_all, cp_all, w_out, l, TM, TN_PROJ)
        nxt2 = (l, 2, mods[l], 4, 3)
        last = l == depth - 1
        nxt_layer = None if last else (l + 1, 0, mods[l + 1], 1, 0)
        split_out = n_p // ROW_BLK if last else 0
        if not is_moe:
            x, h2 = _post(x, mo, norm_g4, l, 1, mods[l], 2, blocks_per_seq, nxt=nxt2)
            act = _gateup(h2, w_gate_dense, w_up_dense, j, TM, TN_FF)
            y = _down(act, w_down_dense, j, TM, TN_DOWN, TK_DOWN)
            outs = _post(x, y, norm_g4, l, 3, mods[l], 5, blocks_per_seq, nxt=nxt_layer, split_out=split_out)
        else:
            x, h2, h2f, ridx, rw = _post(x, mo, norm_g4, l, 1, mods[l], 2, blocks_per_seq, nxt=nxt2,
                                          w_router_pad=w_router_pad[j])
            n_tiles = (m * TOP_K) // TM_MOE + N_EXPERTS
            pos, src, tile_e, tile_rows, n_used = _route_meta(ridx, TM_MOE, n_tiles)
            hs = _dispatch(h2f, src, tile_rows, TM_MOE, DISPATCH_ROWS)
            act = _gateup_grouped(hs, w_gate_moe, w_up_moe, j, tile_e, tile_rows, n_used, TM_MOE, TN_FF)
            ys = _down_grouped(act, w_down_moe, j, tile_e, tile_rows, n_used, TM_MOE, TN_DOWN, TK_DOWN)
            pos_blk = pos.reshape(m // ROW_BLK, ROW_BLK, TOP_K).transpose(0, 2, 1).reshape(m // ROW_BLK, 1, TOP_K * ROW_BLK)
            outs = _post(x, ys, norm_g4, l, 3, mods[l], 5, blocks_per_seq, nxt=nxt_layer, gather=(pos_blk, rw),
                         split_out=split_out)
        if last:
            x = tuple(outs)
        else:
            x, h = outs

    y_prompt = x[0].reshape(bp, t, d)
    y_sample = x[1].reshape(s, bs, d).transpose(1, 0, 2)
    st = jnp.stack
    return (y_prompt, y_sample, st(kp_l), st(vp_l), st(cp_l), st(pp_l), st(ks_l), st(vs_l), st(cs_l), st(ps_l))
```
